```python
import math
import jax, jax.numpy as jnp
from jax import lax
import numpy as np

D_MODEL = 1024
BATCH = 8
SEQ = 2048
DEPTH = 4

GRID_W = 64
CTX_LEN = 256
N_BRANCH = 4
BRANCH_W = D_MODEL // 4
ROPE_THETA = 10000.0
Q_BLOCK = 128
EPS = 1e-6
A_HD = 32
A_VD = 2 * A_HD
A_HEADS = BRANCH_W // A_VD
B_HEADS = 4
B_KV_HEADS = 2
B_HD = BRANCH_W // B_HEADS
S5_H = 16
S5_G = BRANCH_W // S5_H
S5_P = 64
M2_HD = 64
M2_HEADS = BRANCH_W // M2_HD
M2_GROUPS = 2
M2_N = 64
M2_CONV = 5
M2_CHUNK = 128
M2_XBC = BRANCH_W + 2 * M2_GROUPS * M2_N
N_EXPERTS = 16
EXPERT_FF = 2 * D_MODEL
EC_FACTOR = 2
COL_SIZES = (
    A_HEADS * 2 * A_HD, A_HEADS * 2 * A_HD, A_HEADS * A_VD,
    B_HEADS * B_HD, B_KV_HEADS * B_HD, B_KV_HEADS * B_HD,
    BRANCH_W,
    BRANCH_W, M2_XBC, 2 * M2_HEADS,
    N_BRANCH * D_MODEL,
)
IN_COLS = sum(COL_SIZES)
MIX_COLS = IN_COLS - N_BRANCH * D_MODEL

kernel_name = "hybrid_diffusion_parallel_mixers_ec_moe"


def rms_norm(x, g):
    xf = x.astype(jnp.float32)
    xf = xf * lax.rsqrt(jnp.mean(xf * xf, axis=-1, keepdims=True) + EPS)
    return (xf * g.astype(jnp.float32)).astype(x.dtype)


def split_cols(p, sizes):
    return jnp.split(p, np.cumsum(sizes)[:-1].tolist(), axis=-1)


def axial_rope_tables(row, col, hd):
    half = hd // 2
    freqs = 1.0 / (ROPE_THETA ** (jnp.arange(0, half, 2, dtype=jnp.float32) / half))
    def angles(pos):
        ang = pos.astype(jnp.float32)[:, None] * freqs[None, :]
        return jnp.concatenate([ang, ang], axis=-1)
    ang = jnp.concatenate([angles(row), angles(col)], axis=-1)
    return jnp.cos(ang), jnp.sin(ang)


def rotate_half(x):
    x1, x2 = jnp.split(x, 2, axis=-1)
    return jnp.concatenate([-x2, x1], axis=-1)


def apply_rope(x, cos, sin):
    L, hd = cos.shape
    shape = (1, L) + (1,) * (x.ndim - 3) + (hd,)
    cos = cos.reshape(shape).astype(x.dtype)
    sin = sin.reshape(shape).astype(x.dtype)
    xr, xc = jnp.split(x, 2, axis=-1)
    rot = jnp.concatenate([rotate_half(xr), rotate_half(xc)], axis=-1)
    return x * cos + rot * sin


def map_query_blocks(fn, q):
    Bsz, L = q.shape[:2]
    qb = jnp.moveaxis(q.reshape((Bsz, L // Q_BLOCK, Q_BLOCK) + q.shape[2:]), 1, 0)
    out = lax.map(fn, qb)
    return jnp.moveaxis(out, 0, 1).reshape((Bsz, L) + out.shape[3:])


def diff_attn_core(q, k, v, lam):
    s = jnp.einsum('bqhmd,bkhmd->bhmqk', q, k).astype(jnp.float32) / math.sqrt(A_HD)
    p = jax.nn.softmax(s, axis=-1)
    w = p[:, :, 0] - lam * p[:, :, 1]
    return jnp.einsum('bhqk,bkhd->bqhd', w.astype(v.dtype), v)


def diff_attention(q, k, v, qc, kc, vc, cos, sin, q_g, k_g, lam_vec, sub_g, layer_idx, need_ctx):
    lam_init = 0.8 - 0.6 * math.exp(-0.3 * layer_idx)
    lf = lam_vec.astype(jnp.float32)
    lam = jnp.exp(jnp.sum(lf[0] * lf[1])) - jnp.exp(jnp.sum(lf[2] * lf[3])) + lam_init
    def heads(q, k, v):
        Bsz, L = q.shape[:2]
        q = rms_norm(q.reshape(Bsz, L, A_HEADS, 2, A_HD), q_g)
        k = rms_norm(k.reshape(Bsz, L, A_HEADS, 2, A_HD), k_g)
        return q, k, v.reshape(Bsz, L, A_HEADS, A_VD)
    q, k, v = heads(q, k, v)
    qc, kc, vc = heads(qc, kc, vc)
    q = apply_rope(q, cos, sin)
    k = apply_rope(k, cos, sin)
    k_all = jnp.concatenate([kc, k], axis=1)
    v_all = jnp.concatenate([vc, v], axis=1)
    def finish(o):
        o = rms_norm(o, sub_g) * (1.0 - lam_init)
        return o.reshape(o.shape[0], o.shape[1], A_HEADS * A_VD)
    y = finish(map_query_blocks(lambda qb: diff_attn_core(qb, k_all, v_all, lam), q))
    yc = finish(diff_attn_core(qc, kc, vc, lam)) if need_ctx else None
    return y, yc


def gqa_core(q, k, v):
    s = jnp.einsum('bqkgd,bskd->bkgqs', q, k).astype(jnp.float32) / math.sqrt(B_HD)
    p = jax.nn.softmax(s, axis=-1)
    return jnp.einsum('bkgqs,bskd->bqkgd', p.astype(v.dtype), v)


def gqa_attention(q, k, v, qc, kc, vc, cos, sin, q_g, k_g, need_ctx):
    def heads(q, k, v):
        Bsz, L = q.shape[:2]
        q = rms_norm(q.reshape(Bsz, L, B_HEADS, B_HD), q_g)
        k = rms_norm(k.reshape(Bsz, L, B_KV_HEADS, B_HD), k_g)
        return q, k, v.reshape(Bsz, L, B_KV_HEADS, B_HD)
    q, k, v = heads(q, k, v)
    qc, kc, vc = heads(qc, kc, vc)
    q = apply_rope(q, cos, sin)
    k = apply_rope(k, cos, sin)
    k_all = jnp.concatenate([kc, k], axis=1)
    v_all = jnp.concatenate([vc, v], axis=1)
    group = lambda t: t.reshape(t.shape[0], t.shape[1], B_KV_HEADS, B_HEADS // B_KV_HEADS, B_HD)
    flat = lambda o: o.reshape(o.shape[0], o.shape[1], BRANCH_W)
    y = flat(map_query_blocks(lambda qb: gqa_core(qb, k_all, v_all), group(q)))
    yc = flat(gqa_core(group(qc), kc, vc)) if need_ctx else None
    return y, yc


def s5_discretise(a_re, a_im, log_dt, b_re, b_im):
    lam = lax.complex(a_re.astype(jnp.float32), a_im.astype(jnp.float32))
    dt = jnp.exp(log_dt.astype(jnp.float32))[:, None]
    lam_bar = jnp.exp(lam * dt)
    bmat = lax.complex(b_re.astype(jnp.float32), b_im.astype(jnp.float32))
    return lam_bar, ((lam_bar - 1.0) / lam)[..., None] * bmat


def diag_scan(lam_bar, bu, h0, reverse):
    a = jnp.broadcast_to(lam_bar, bu.shape)
    def combine(e1, e2):
        a1, b1 = e1
        a2, b2 = e2
        return a2 * a1, a2 * b1 + b2
    a_cum, h = lax.associative_scan(combine, (a, bu), axis=1, reverse=reverse)
    if h0 is not None:
        h = h + a_cum * h0[:, None]
    return h


def s5_mixer(u, uc, a_re, a_im, log_dt, b_re, b_im, c_re, c_im, d_skip, w_glu, b_glu, need_ctx):
    def groups(t):
        return t.astype(jnp.float32).reshape(t.shape[0], t.shape[1], S5_G, S5_H)
    ug, ucg = groups(u), groups(uc)
    d = d_skip.astype(jnp.float32).reshape(S5_G, S5_H)
    y = d * ug
    yc = d * ucg if need_ctx else None
    for direction in range(2):
        lam_bar, b_bar = s5_discretise(a_re[direction], a_im[direction], log_dt[direction],
                                       b_re[direction], b_im[direction])
        cmat = lax.complex(c_re[direction].astype(jnp.float32), c_im[direction].astype(jnp.float32))
        rev = direction == 1
        hc = diag_scan(lam_bar, jnp.einsum('gph,blgh->blgp', b_bar, ucg), None, rev)
        h0 = hc[:, 0] if rev else hc[:, -1]
        h = diag_scan(lam_bar, jnp.einsum('gph,blgh->blgp', b_bar, ug), h0, rev)
        y = y + jnp.real(jnp.einsum('ghp,blgp->blgh', cmat, h))
        if need_ctx:
            yc = yc + jnp.real(jnp.einsum('ghp,blgp->blgh', cmat, hc))
    def glu(t, like):
        t = jax.nn.gelu(t.reshape(t.shape[0], t.shape[1], BRANCH_W))
        return (t * jax.nn.sigmoid(t @ w_glu.astype(jnp.float32) + b_glu.astype(jnp.float32))).astype(like.dtype)
    return glu(y, u), (glu(yc, uc) if need_ctx else None)


def dwconv_centred(x, w, b):
    K = w.shape[0]
    y = lax.conv_general_dilated(x, w[:, None, :], window_strides=(1,), padding=[(K // 2, K // 2)],
                                 dimension_numbers=('NWC', 'WIO', 'NWC'), feature_group_count=x.shape[-1])
    return y + b


def segsum(v):
    T = v.shape[-1]
    cs = jnp.cumsum(v, axis=-1)
    diff = cs[..., :, None] - cs[..., None, :]
    return jnp.where(jnp.tril(jnp.ones((T, T), dtype=bool)), diff, -jnp.inf)


def ssd(xs, dt, bm, cm, a, h0, need_y):
    Bsz, L, H, P = xs.shape
    N = bm.shape[-1]
    nc = L // M2_CHUNK
    xd = (xs * dt[..., None]).reshape(Bsz, nc, M2_CHUNK, H, P)
    ad = jnp.moveaxis((dt * a).reshape(Bsz, nc, M2_CHUNK, H), 3, 2)
    bc = bm.reshape(Bsz, nc, M2_CHUNK, H, N)
    cc = cm.reshape(Bsz, nc, M2_CHUNK, H, N)
    a_cs = jnp.cumsum(ad, axis=-1)
    decay_to_end = jnp.exp(a_cs[..., -1:] - a_cs)
    chunk_states = jnp.einsum('bcshn,bchs,bcshp->bchpn', bc, decay_to_end, xd)
    if h0 is None:
        h0 = jnp.zeros((Bsz, H, P, N), jnp.float32)
    states = jnp.concatenate([h0[:, None], chunk_states], axis=1)
    a_last = jnp.pad(a_cs[..., -1], ((0, 0), (1, 0), (0, 0)))
    decay_chunk = jnp.exp(segsum(jnp.moveaxis(a_last, 1, 2)))
    new_states = jnp.einsum('bhzc,bchpn->bzhpn', decay_chunk, states)
    final = new_states[:, -1]
    if not need_y:
        return None, final
    states_in = new_states[:, :-1]
    lmat = jnp.exp(segsum(ad))
    y_diag = jnp.einsum('bclhn,bcshn,bchls,bcshp->bclhp', cc, bc, lmat, xd)
    y_off = jnp.einsum('bclhn,bchpn,bchl->bclhp', cc, states_in, jnp.exp(a_cs))
    return (y_diag + y_off).reshape(Bsz, L, H, P), final


def m2_prep(xbc, dt_raw, conv_w, conv_b):
    Bsz, L = xbc.shape[:2]
    xbc = jax.nn.silu(dwconv_centred(xbc, conv_w, conv_b))
    xs, bm, cm = jnp.split(xbc, [BRANCH_W, BRANCH_W + M2_GROUPS * M2_N], axis=-1)
    rep = M2_HEADS // M2_GROUPS
    xs = xs.reshape(Bsz, L, M2_HEADS, M2_HD).astype(jnp.float32)
    bm = jnp.repeat(bm.reshape(Bsz, L, M2_GROUPS, M2_N), rep, axis=2).astype(jnp.float32)
    cm = jnp.repeat(cm.reshape(Bsz, L, M2_GROUPS, M2_N), rep, axis=2).astype(jnp.float32)
    dt_raw = dt_raw.reshape(Bsz, L, 2, M2_HEADS).astype(jnp.float32)
    return xs, bm, cm, dt_raw


def mamba2_mixer(z, xbc, dt_raw, zc, xbcc, dtc_raw, conv_w, conv_b, a_log, dt_bias, d_skip, norm_g, need_ctx):
    lat = m2_prep(xbc, dt_raw, conv_w, conv_b)
    cxt = m2_prep(xbcc, dtc_raw, conv_w, conv_b)
    d = d_skip.astype(jnp.float32)[:, None]
    y = d * lat[0]
    yc = d * cxt[0] if need_ctx else None
    flip = lambda t: jnp.flip(t, axis=1)
    for direction in range(2):
        a = -jnp.exp(a_log[direction].astype(jnp.float32))
        bias = dt_bias[direction].astype(jnp.float32)
        def seqs(s):
            xs, bm, cm, dtr = s
            out = (xs, jax.nn.softplus(dtr[:, :, direction] + bias), bm, cm)
            return tuple(flip(t) for t in out) if direction == 1 else out
        yc_d, hc = ssd(*seqs(cxt), a, None, need_ctx)
        y_d, _ = ssd(*seqs(lat), a, hc, True)
        y = y + (flip(y_d) if direction == 1 else y_d)
        if need_ctx:
            yc = yc + (flip(yc_d) if direction == 1 else yc_d)
    def gate_norm(t, zz):
        g = t.reshape(zz.shape[0], zz.shape[1], BRANCH_W) * jax.nn.silu(zz.astype(jnp.float32))
        return rms_norm(g, norm_g).astype(zz.dtype)
    return gate_norm(y, z), (gate_norm(yc, zc) if need_ctx else None)


def gated_merge(branches, gate_logits, w_branch, w_out):
    y = jnp.stack(branches, axis=2)
    proj = jnp.einsum('blnw,nwd->blnd', y, w_branch)
    g = jax.nn.sigmoid(gate_logits.reshape(proj.shape))
    return jnp.sum(g * proj, axis=2) @ w_out


def expert_choice_ffn(h, w_router, w_up, w_gate, w_down):
    Bsz, N, D = h.shape
    cap = EC_FACTOR * N // N_EXPERTS
    aff = jax.nn.softmax((h @ w_router).astype(jnp.float32), axis=-1)
    gate, idx = lax.top_k(jnp.swapaxes(aff, 1, 2), cap)
    xs = jax.vmap(lambda hb, ib: hb[ib])(h, idx)
    u = jnp.einsum('becd,edf->becf', xs, w_up)
    v = jnp.einsum('becd,edf->becf', xs, w_gate)
    y = jnp.einsum('becf,efd->becd', jax.nn.silu(v) * u, w_down) * gate[..., None].astype(h.dtype)
    return jax.vmap(lambda yb, ib: jnp.zeros((N, D), yb.dtype).at[ib.reshape(-1)].add(yb.reshape(-1, D)))(y, idx)


def setup_inputs(seed: int = 0) -> dict:
    key = jax.random.key(seed)
    ks = iter(jax.random.split(key, 48))
    f32 = jnp.float32
    D = D_MODEL
    nrm = lambda shape, scale: scale * jax.random.normal(next(ks), shape, f32)
    unif = lambda shape, lo, hi: jax.random.uniform(next(ks), shape, f32, lo, hi)
    n = jnp.arange(S5_P, dtype=f32)
    m2_dt = jnp.exp(unif((DEPTH, 2, M2_HEADS), math.log(1e-3), math.log(1e-1)))
    return {
        'x': nrm((BATCH, SEQ, D), 1.0),
        'c': nrm((BATCH, D), 1.0),
        'ctx': nrm((BATCH, CTX_LEN, D), 1.0),
        'c_ctx': nrm((D,), 1.0),
        'w_ada': nrm((DEPTH, D, 6 * D), 0.5 * D ** -0.5),
        'b_ada': nrm((DEPTH, 6 * D), 0.02),
        'norm_g': 1.0 + nrm((DEPTH, 2, D), 0.02),
        'w_in': nrm((DEPTH, D, IN_COLS), D ** -0.5),
        'da_q_g': 1.0 + nrm((DEPTH, A_HD), 0.02),
        'da_k_g': 1.0 + nrm((DEPTH, A_HD), 0.02),
        'da_lambda': nrm((DEPTH, 4, A_HD), 0.1),
        'da_sub_g': 1.0 + nrm((DEPTH, A_VD), 0.02),
        'gqa_q_g': 1.0 + nrm((DEPTH, B_HD), 0.02),
        'gqa_k_g': 1.0 + nrm((DEPTH, B_HD), 0.02),
        's5_a_re': -0.5 + nrm((DEPTH, 2, S5_G, S5_P), 0.01),
        's5_a_im': math.pi * n + nrm((DEPTH, 2, S5_G, S5_P), 0.01),
        's5_log_dt': unif((DEPTH, 2, S5_G), math.log(1e-3), math.log(1e-1)),
        's5_b_re': nrm((DEPTH, 2, S5_G, S5_P, S5_H), (2 * S5_H) ** -0.5),
        's5_b_im': nrm((DEPTH, 2, S5_G, S5_P, S5_H), (2 * S5_H) ** -0.5),
        's5_c_re': nrm((DEPTH, 2, S5_G, S5_H, S5_P), (2 * S5_P) ** -0.5),
        's5_c_im': nrm((DEPTH, 2, S5_G, S5_H, S5_P), (2 * S5_P) ** -0.5),
        's5_d': nrm((DEPTH, BRANCH_W), 1.0),
        's5_w_glu': nrm((DEPTH, BRANCH_W, BRANCH_W), BRANCH_W ** -0.5),
        's5_b_glu': nrm((DEPTH, BRANCH_W), 0.02),
        'm2_conv_w': nrm((DEPTH, M2_CONV, M2_XBC), M2_CONV ** -0.5),
        'm2_conv_b': nrm((DEPTH, M2_XBC), 0.02),
        'm2_a_log': jnp.log(unif((DEPTH, 2, M2_HEADS), 1.0, 16.0)),
        'm2_dt_bias': m2_dt + jnp.log(-jnp.expm1(-m2_dt)),
        'm2_d': 1.0 + nrm((DEPTH, M2_HEADS), 0.02),
        'm2_norm_g': 1.0 + nrm((DEPTH, BRANCH_W), 0.02),
        'w_branch': nrm((DEPTH, N_BRANCH, BRANCH_W, D), BRANCH_W ** -0.5),
        'w_out': nrm((DEPTH, D, D), D ** -0.5),
        'w_router': nrm((DEPTH, D, N_EXPERTS), D ** -0.5),
        'w_up': nrm((DEPTH, N_EXPERTS, D, EXPERT_FF), D ** -0.5),
        'w_gate': nrm((DEPTH, N_EXPERTS, D, EXPERT_FF), D ** -0.5),
        'w_down': nrm((DEPTH, N_EXPERTS, EXPERT_FF, D), EXPERT_FF ** -0.5),
    }


def reference(x, c, ctx, c_ctx, w_ada, b_ada, norm_g, w_in, da_q_g, da_k_g, da_lambda, da_sub_g,
              gqa_q_g, gqa_k_g, s5_a_re, s5_a_im, s5_log_dt, s5_b_re, s5_b_im, s5_c_re, s5_c_im,
              s5_d, s5_w_glu, s5_b_glu, m2_conv_w, m2_conv_b, m2_a_log, m2_dt_bias, m2_d, m2_norm_g,
              w_branch, w_out, w_router, w_up, w_gate, w_down):
    Bsz, L, D = x.shape
    rows = L // GRID_W
    row = jnp.repeat(jnp.arange(rows, dtype=jnp.int32), GRID_W)
    col = jnp.tile(jnp.arange(GRID_W, dtype=jnp.int32), rows)
    cos_a, sin_a = axial_rope_tables(row, col, A_HD)
    cos_b, sin_b = axial_rope_tables(row, col, B_HD)
    sc = jax.nn.silu(c)
    scc = jax.nn.silu(c_ctx)
    xc = ctx
    for l in range(DEPTH):
        need_ctx = l < DEPTH - 1
        mod = (sc @ w_ada[l] + b_ada[l]).reshape(Bsz, 6, 1, D)
        modc = (scc @ w_ada[l] + b_ada[l]).reshape(6, D)
        h = rms_norm(x, norm_g[l, 0]) * (1.0 + mod[:, 1]) + mod[:, 0]
        hc = rms_norm(xc, norm_g[l, 0]) * (1.0 + modc[1]) + modc[0]
        p = split_cols(h @ w_in[l], COL_SIZES)
        if need_ctx:
            pc = split_cols(hc @ w_in[l], COL_SIZES)
        else:
            pc = split_cols(hc @ w_in[l, :, :MIX_COLS], COL_SIZES[:-1])
        ya, yac = diff_attention(p[0], p[1], p[2], pc[0], pc[1], pc[2], cos_a, sin_a,
                                 da_q_g[l], da_k_g[l], da_lambda[l], da_sub_g[l], l, need_ctx)
        yb, ybc = gqa_attention(p[3], p[4], p[5], pc[3], pc[4], pc[5], cos_b, sin_b,
                                gqa_q_g[l], gqa_k_g[l], need_ctx)
        ys, ysc = s5_mixer(p[6], pc[6], s5_a_re[l], s5_a_im[l], s5_log_dt[l], s5_b_re[l], s5_b_im[l],
                           s5_c_re[l], s5_c_im[l], s5_d[l], s5_w_glu[l], s5_b_glu[l], need_ctx)
        ym, ymc = mamba2_mixer(p[7], p[8], p[9], pc[7], pc[8], pc[9], m2_conv_w[l], m2_conv_b[l],
                               m2_a_log[l], m2_dt_bias[l], m2_d[l], m2_norm_g[l], need_ctx)
        x = x + mod[:, 2] * gated_merge((ya, yb, ys, ym), p[10], w_branch[l], w_out[l])
        if need_ctx:
            xc = xc + modc[2] * gated_merge((yac, ybc, ysc, ymc), pc[10], w_branch[l], w_out[l])
        h = rms_norm(x, norm_g[l, 1]) * (1.0 + mod[:, 4]) + mod[:, 3]
        x = x + mod[:, 5] * expert_choice_ffn(h, w_router[l], w_up[l], w_gate[l], w_down[l])
        if need_ctx:
            hc = rms_norm(xc, norm_g[l, 1]) * (1.0 + modc[4]) + modc[3]
            xc = xc + modc[5] * expert_choice_ffn(hc, w_router[l], w_up[l], w_gate[l], w_down[l])
    return x
```

```python
import functools
import math

import jax
import jax.numpy as jnp
import numpy as np
from jax import lax
from jax.experimental import pallas as pl
from jax.experimental.pallas import tpu as pltpu

F32 = jnp.float32
BF16 = jnp.bfloat16
HI = lax.Precision.HIGHEST

D_MODEL = 1024
SEQ = 2048
DEPTH = 4
GRID_W = 64
CTX_LEN = 256
T_ALL = CTX_LEN + SEQ
BRANCH_W = 256
ROPE_THETA = 10000.0
EPS = 1e-6
A_HD = 32
A_VD = 64
A_HEADS = 4
B_HEADS = 4
B_HD = 64
S5_H = 16
S5_G = 16
S5_P = 64
S5_STATE = S5_G * S5_P
M2_HD = 64
M2_HEADS = 4
M2_GROUPS = 2
M2_N = 64
M2_CONV = 5
M2_XBC = 512
N_EXPERTS = 16
EXPERT_FF = 2 * D_MODEL
EC_FACTOR = 2
CAP_CTX = EC_FACTOR * CTX_LEN // N_EXPERTS
CAP_LAT = EC_FACTOR * SEQ // N_EXPERTS
CAP_ALL = CAP_CTX + CAP_LAT

LANES = 128
TOK_TILE = 256
N_TOK_TILES = T_ALL // TOK_TILE
CHUNK = 128
N_CHUNKS = T_ALL // CHUNK
CTX_CHUNKS = CTX_LEN // CHUNK
MIX_COLS = 2944
FF_TILE = 512
VMEM_LIMIT = 56 * 1024 * 1024
LOG2E = 1.0 / math.log(2.0)
SAFE_LOGIT = 40.0


def _params(sem, vmem=VMEM_LIMIT):
    return pltpu.CompilerParams(dimension_semantics=sem, vmem_limit_bytes=vmem)


def _sigmoid(x):
    return 1.0 / (1.0 + jnp.exp(-x))


def _silu(x):
    return x * _sigmoid(x)


def _rms_rows(x):
    return x * lax.rsqrt(jnp.mean(x * x, axis=-1, keepdims=True) + EPS)


def _ada_kernel(c_ref, w_ref, b_ref, o_ref):
    sc = _silu(c_ref[...])
    o_ref[0] = jnp.dot(sc, w_ref[0], precision=HI, preferred_element_type=F32) + b_ref[0]


def _ada_call(cc, w_ada, b_ada):
    tn = 1536
    return pl.pallas_call(
        _ada_kernel,
        grid=(DEPTH, 6 * D_MODEL // tn),
        in_specs=[
            pl.BlockSpec((16, D_MODEL), lambda l, j: (0, 0)),
            pl.BlockSpec((1, D_MODEL, tn), lambda l, j: (l, 0, j)),
            pl.BlockSpec((1, 1, tn), lambda l, j: (l, 0, j)),
        ],
        out_specs=pl.BlockSpec((1, 16, tn), lambda l, j: (l, 0, j)),
        out_shape=jax.ShapeDtypeStruct((DEPTH, 16, 6 * D_MODEL), F32),
        compiler_params=_params(("arbitrary", "arbitrary")),
        name="ada_mod",
    )(cc, w_ada, b_ada.reshape(DEPTH, 1, 6 * D_MODEL))


def _group_mean(x, gmat):
    hi = x.astype(BF16)
    lo = (x - hi.astype(F32)).astype(BF16)
    return jnp.dot(hi, gmat, preferred_element_type=F32) + jnp.dot(lo, gmat, preferred_element_type=F32)


def _group_norm_rope(x, gmat, gain, cos, sin, shift):
    xn = x * lax.rsqrt(_group_mean(x * x, gmat) + EPS) * gain
    lane = lax.broadcasted_iota(jnp.int32, xn.shape, 1)
    up = pltpu.roll(xn, BRANCH_W - shift, 1)
    dn = pltpu.roll(xn, shift, 1)
    rot = jnp.where((lane % (2 * shift)) < shift, -up, dn)
    return xn * cos + rot * sin


def _inproj_kernel(x_ref, mod_ref, g_ref, w_ref, gains_ref, ga_ref, gb_ref,
                   cosa_ref, sina_ref, cosb_ref, sinb_ref,
                   qa_ref, ka_ref, va_ref, qb_ref, kb_ref, vb_ref, u_ref, z_ref, xbc_ref, dt_ref):
    mod = mod_ref[0, 0]
    h = _rms_rows(x_ref[0]) * g_ref[0] * (1.0 + mod[1:2]) + mod[0:1]
    p = jnp.dot(h.astype(BF16), w_ref[0], preferred_element_type=F32)
    gains = gains_ref[0]
    w = BRANCH_W
    qa = _group_norm_rope(p[:, 0:w], ga_ref[...], gains[0:1], cosa_ref[...], sina_ref[...], A_HD // 4)
    ka = _group_norm_rope(p[:, w:2 * w], ga_ref[...], gains[1:2], cosa_ref[...], sina_ref[...], A_HD // 4)
    qb = _group_norm_rope(p[:, 3 * w:4 * w], gb_ref[...], gains[2:3], cosb_ref[...], sinb_ref[...], B_HD // 4)
    kb = _group_norm_rope(p[:, 4 * w:5 * w], gb_ref[...], gains[3:4], cosb_ref[...], sinb_ref[...], B_HD // 4)
    qa_ref[0] = (qa * (LOG2E / math.sqrt(A_HD))).astype(BF16)
    ka_ref[0] = ka.astype(BF16)
    va_ref[0] = p[:, 2 * w:3 * w].astype(BF16)
    qb_ref[0] = (qb * (LOG2E / math.sqrt(B_HD))).astype(BF16)
    kb_ref[0] = kb.astype(BF16)
    vb_ref[0] = p[:, 5 * w:6 * w].astype(BF16)
    u_ref[...] = p[:, 6 * w:7 * w].astype(BF16)
    z_ref[0] = p[:, 7 * w:8 * w]
    xbc_ref[0] = p[:, 8 * w:10 * w]
    dt_ref[0] = p[:, 10 * w:10 * w + LANES]


def _inproj_call(l, x, modsel, norm_g, w_mix, gains, gmat_a, gmat_b, cosa, sina, cosb, sinb):
    bsz = x.shape[0]
    w = BRANCH_W
    tok = lambda n: pl.BlockSpec((1, TOK_TILE, n), lambda b, i: (b, i, 0))
    tab = pl.BlockSpec((TOK_TILE, w), lambda b, i: (i, 0))
    full2 = lambda a: pl.BlockSpec(a.shape, lambda b, i: (0, 0))
    out_bt = lambda n, dt: jax.ShapeDtypeStruct((bsz, T_ALL, n), dt)
    return pl.pallas_call(
        _inproj_kernel,
        grid=(bsz, N_TOK_TILES),
        in_specs=[
            tok(D_MODEL),
            pl.BlockSpec((1, 1, 8, D_MODEL), lambda b, i: (b, jnp.minimum(i, 1), 0, 0)),
            pl.BlockSpec((1, 1, D_MODEL), lambda b, i: (2 * l, 0, 0)),
            pl.BlockSpec((1, D_MODEL, MIX_COLS), lambda b, i: (l, 0, 0)),
            pl.BlockSpec((1, 8, w), lambda b, i: (l, 0, 0)),
            full2(gmat_a), full2(gmat_b), tab, tab, tab, tab,
        ],
        out_specs=[tok(w), tok(w), tok(w), tok(w), tok(w), tok(w),
                   pl.BlockSpec((TOK_TILE, w), lambda b, i: (i, b)),
                   tok(w), tok(M2_XBC), tok(LANES)],
        out_shape=[out_bt(w, BF16)] * 6 + [jax.ShapeDtypeStruct((T_ALL, bsz * w), BF16),
                                            out_bt(w, F32), out_bt(M2_XBC, F32), out_bt(LANES, F32)],
        compiler_params=_params(("parallel", "parallel")),
        name="in_proj",
    )(x, modsel, norm_g, w_mix, gains, gmat_a, gmat_b, cosa, sina, cosb, sinb)


def _attn_kernel(bound_ref, q_ref, k_ref, v_ref, lam_ref, subg_ref, gmat_ref, o_ref, *,
                 n_maps, post_scale, bound_idx):
    lane = lax.broadcasted_iota(jnp.int32, (TOK_TILE, BRANCH_W), 1)
    q = q_ref[0]
    map_w = A_VD // n_maps

    def attend(n_keys, shift):
        k = k_ref[0, 0:n_keys, :]
        v = v_ref[0, 0:n_keys, :]
        out = jnp.zeros((TOK_TILE, BRANCH_W), F32)
        for head in range(A_HEADS):
            acc = None
            for m in range(n_maps):
                lo = head * A_VD + m * map_w
                qm = jnp.where((lane >= lo) & (lane < lo + map_w), q, jnp.zeros_like(q))
                s = lax.dot_general(qm, k, (((1,), (1,)), ((), ())), preferred_element_type=F32)
                if shift:
                    s = s - jnp.max(s, axis=-1, keepdims=True)
                e = jnp.exp2(s)
                r = 1.0 / jnp.sum(e, axis=-1, keepdims=True)
                o = jnp.dot(e.astype(BF16), v, preferred_element_type=F32)
                acc = o * r if m == 0 else acc - o * (r * lam_ref[0])
            out = jnp.where((lane >= head * A_VD) & (lane < (head + 1) * A_VD), acc, out)
        if post_scale is not None:
            out = out * lax.rsqrt(_group_mean(out * out, gmat_ref[...]) + EPS) * subg_ref[0] * post_scale
        o_ref[0] = out.astype(BF16)

    is_ctx = pl.program_id(1) == 0
    small = bound_ref[bound_idx] <= SAFE_LOGIT
    for n_keys, tile_sel in ((CTX_LEN, is_ctx), (T_ALL, jnp.logical_not(is_ctx))):
        for shift, bound_sel in ((False, small), (True, jnp.logical_not(small))):
            pl.when(jnp.logical_and(tile_sel, bound_sel))(functools.partial(attend, n_keys, shift))


def _attn_call(l, bounds, q, k, v, lam, subg, gmat, n_maps, post_scale, name):
    bsz = q.shape[0]
    w = BRANCH_W
    return pl.pallas_call(
        functools.partial(_attn_kernel, n_maps=n_maps, post_scale=post_scale,
                          bound_idx=2 * l + (0 if n_maps == 2 else 1)),
        grid=(bsz, N_TOK_TILES),
        in_specs=[
            pl.BlockSpec(memory_space=pltpu.SMEM),
            pl.BlockSpec((1, TOK_TILE, w), lambda b, i: (b, i, 0)),
            pl.BlockSpec((1, T_ALL, w), lambda b, i: (b, 0, 0)),
            pl.BlockSpec((1, T_ALL, w), lambda b, i: (b, 0, 0)),
            pl.BlockSpec((1, 1, 1), lambda b, i: (l, 0, 0)),
            pl.BlockSpec((1, 1, w), lambda b, i: (l, 0, 0)),
            pl.BlockSpec(gmat.shape, lambda b, i: (0, 0)),
        ],
        out_specs=pl.BlockSpec((1, TOK_TILE, w), lambda b, i: (b, i, 0)),
        out_shape=jax.ShapeDtypeStruct((bsz, T_ALL, w), BF16),
        compiler_params=_params(("parallel", "parallel")),
        name=name,
    )(bounds, q, k, v, lam, subg, gmat)


def _scan_chunk_index(direction, j):
    back = jnp.where(j < CTX_CHUNKS, CTX_CHUNKS - 1 - j, N_CHUNKS + CTX_CHUNKS - 1 - j)
    return jnp.where(direction == 0, j, back)


def _s5_kernel(u_ref, bmat_ref, lam_ref, cmat_ref, y_ref, bu_scr, h_scr, *, bsz):
    direction = pl.program_id(0)

    @pl.when(pl.program_id(1) == 0)
    def _():
        h_scr[...] = jnp.zeros_like(h_scr)

    bu_scr[...] = jnp.dot(u_ref[...], bmat_ref[0], preferred_element_type=F32)
    lam_re = lam_ref[0, :, 0:S5_STATE]
    lam_im = lam_ref[0, :, S5_STATE:2 * S5_STATE]

    def step(t, carry):
        h_re, h_im = carry
        tt = jnp.where(direction == 0, t, CHUNK - 1 - t)
        rows = pl.ds(pl.multiple_of(tt * bsz, bsz), bsz)
        n_re = lam_re * h_re - lam_im * h_im + bu_scr[rows, 0:S5_STATE]
        n_im = lam_re * h_im + lam_im * h_re + bu_scr[rows, S5_STATE:2 * S5_STATE]
        bu_scr[rows, 0:S5_STATE] = n_re
        bu_scr[rows, S5_STATE:2 * S5_STATE] = n_im
        return n_re, n_im

    h_re, h_im = lax.fori_loop(0, CHUNK, step, (h_scr[:, 0:S5_STATE], h_scr[:, S5_STATE:2 * S5_STATE]),
                               unroll=4)
    h_scr[:, 0:S5_STATE] = h_re
    h_scr[:, S5_STATE:2 * S5_STATE] = h_im
    y_ref[0] = jnp.dot(bu_scr[...].astype(BF16), cmat_ref[0], preferred_element_type=F32)


def _s5_call(u_tb, bmat, lam, cmat, bsz):
    rows = CHUNK * bsz
    return pl.pallas_call(
        functools.partial(_s5_kernel, bsz=bsz),
        grid=(2, N_CHUNKS),
        in_specs=[
            pl.BlockSpec((rows, BRANCH_W), lambda d, j: (_scan_chunk_index(d, j), 0)),
            pl.BlockSpec((1, BRANCH_W, 2 * S5_STATE), lambda d, j: (d, 0, 0)),
            pl.BlockSpec((1, bsz, 2 * S5_STATE), lambda d, j: (d, 0, 0)),
            pl.BlockSpec((1, 2 * S5_STATE, BRANCH_W), lambda d, j: (d, 0, 0)),
        ],
        out_specs=pl.BlockSpec((1, rows, BRANCH_W), lambda d, j: (d, _scan_chunk_index(d, j), 0)),
        out_shape=jax.ShapeDtypeStruct((2, T_ALL * bsz, BRANCH_W), F32),
        scratch_shapes=[pltpu.VMEM((rows, 2 * S5_STATE), F32), pltpu.VMEM((bsz, 2 * S5_STATE), F32)],
        compiler_params=_params(("arbitrary", "arbitrary")),
        name="s5_scan",
    )(u_tb, bmat, lam, cmat)


def _s5_glu_kernel(u_ref, y_ref, d_ref, w_ref, b_ref, o_ref):
    t = d_ref[0] * u_ref[...].astype(F32) + y_ref[0] + y_ref[1]
    t = 0.5 * t * (1.0 + jnp.tanh(math.sqrt(2.0 / math.pi) * (t + 0.044715 * (t * t * t))))
    gate = jnp.dot(t.astype(BF16), w_ref[0], preferred_element_type=F32) + b_ref[0]
    o_ref[...] = (t * _sigmoid(gate)).astype(BF16)


def _s5_glu_call(l, u_tb, y, s5_d, w_glu, b_glu, bsz):
    rows = CHUNK * bsz
    w = BRANCH_W
    return pl.pallas_call(
        _s5_glu_kernel,
        grid=(N_CHUNKS,),
        in_specs=[
            pl.BlockSpec((rows, w), lambda j: (j, 0)),
            pl.BlockSpec((2, rows, w), lambda j: (0, j, 0)),
            pl.BlockSpec((1, 1, w), lambda j: (l, 0, 0)),
            pl.BlockSpec((1, w, w), lambda j: (l, 0, 0)),
            pl.BlockSpec((1, 1, w), lambda j: (l, 0, 0)),
        ],
        out_specs=pl.BlockSpec((rows, w), lambda j: (j, 0)),
        out_shape=jax.ShapeDtypeStruct((T_ALL * bsz, w), BF16),
        compiler_params=_params(("parallel",)),
        name="s5_glu",
    )(u_tb, y, s5_d, w_glu, b_glu)


PAD_ROWS = 8


def _softplus(x):
    return jnp.maximum(x, 0.0) + jnp.log(1.0 + jnp.exp(-jnp.abs(x)))


def _ssd_kernel(*refs, direction, final):
    if final:
        (xbc_ref, dt_ref, cw_ref, cb_ref, a_ref, bias_ref, dskip_ref, z_ref, yprev_ref, ng_ref,
         o_ref, pad_scr, xact_scr, dtv_scr, ad_scr, st_scr) = refs
    else:
        (xbc_ref, dt_ref, cw_ref, cb_ref, a_ref, bias_ref, dskip_ref,
         o_ref, pad_scr, xact_scr, dtv_scr, ad_scr, st_scr) = refs
    rev = direction == 1

    lat0 = CTX_LEN + 3 * PAD_ROWS - PAD_ROWS
    pad_scr[...] = jnp.zeros_like(pad_scr)
    pad_scr[PAD_ROWS:PAD_ROWS + CTX_LEN, :] = xbc_ref[0, 0:CTX_LEN, :]
    pad_scr[lat0:lat0 + SEQ, :] = xbc_ref[0, CTX_LEN:T_ALL, :]
    for c in range(N_CHUNKS):
        base = c * CHUNK + (PAD_ROWS if c < CTX_CHUNKS else lat0 - CTX_LEN)
        acc = jnp.broadcast_to(cb_ref[0], (CHUNK, M2_XBC))
        for tap in range(M2_CONV):
            start = base + tap - M2_CONV // 2
            acc = acc + cw_ref[0, tap:tap + 1, :] * pad_scr[start:start + CHUNK, :]
        xact_scr[c * CHUNK:(c + 1) * CHUNK, :] = _silu(acc)

    dtv = _softplus(dt_ref[0] + bias_ref[0])
    dtv_scr[...] = dtv
    ad_scr[...] = dtv * a_ref[0]
    st_scr[...] = jnp.zeros_like(st_scr)

    ti = lax.broadcasted_iota(jnp.int32, (CHUNK, CHUNK), 0)
    si = lax.broadcasted_iota(jnp.int32, (CHUNK, CHUNK), 1)
    tri = jnp.where(ti >= si, 1.0, 0.0).astype(F32)
    keep = (si >= ti) if rev else (ti >= si)

    def chunk_body(j, carry):
        ci = _scan_chunk_index(direction, j)
        rows = pl.ds(pl.multiple_of(ci * CHUNK, CHUNK), CHUNK)
        xa = xact_scr[rows, :]
        dtc = dtv_scr[rows, :]
        adc = ad_scr[rows, :]
        cs = jnp.dot(tri, adc, precision=HI, preferred_element_type=F32)
        tot = cs[CHUNK - 1:CHUNK, :]
        pcs = cs - adc if rev else cs
        pcs_t = jnp.transpose(pcs)
        ys = []
        for grp in range(M2_GROUPS):
            b0 = BRANCH_W + grp * M2_N
            c0 = BRANCH_W + M2_GROUPS * M2_N + grp * M2_N
            bg = xa[:, b0:b0 + M2_N]
            cg = xa[:, c0:c0 + M2_N].astype(BF16)
            gmat = lax.dot_general(cg, bg.astype(BF16), (((1,), (1,)), ((), ())), preferred_element_type=F32)
            for hh in range(M2_HEADS // M2_GROUPS):
                head = grp * (M2_HEADS // M2_GROUPS) + hh
                ch = direction * M2_HEADS + head
                col = pcs[:, ch:ch + 1]
                row = pcs_t[ch:ch + 1, :]
                tot_h = tot[:, ch:ch + 1]
                arg = (row - col) if rev else (col - row)
                lmat = jnp.exp(jnp.where(keep, arg, -jnp.inf))
                xd = (xa[:, head * M2_HD:(head + 1) * M2_HD] * dtc[:, ch:ch + 1]).astype(BF16)
                y_diag = jnp.dot((gmat * lmat).astype(BF16), xd, preferred_element_type=F32)
                off = jnp.exp(tot_h - col) if rev else jnp.exp(col)
                dte = jnp.exp(col) if rev else jnp.exp(tot_h - col)
                state = st_scr[head]
                y_off = jnp.dot(cg, state.astype(BF16), preferred_element_type=F32) * off
                st_scr[head] = jnp.exp(tot_h) * state + lax.dot_general(
                    (bg * dte).astype(BF16), xd, (((0,), (0,)), ((), ())), preferred_element_type=F32)
                ys.append(y_diag + y_off)
        y = jnp.concatenate(ys, axis=-1)
        if final:
            y = y + yprev_ref[0, rows, :]
            g = y * _silu(z_ref[0, rows, :])
            o_ref[0, rows, :] = (_rms_rows(g) * ng_ref[0]).astype(BF16)
        else:
            o_ref[0, rows, :] = y + dskip_ref[0] * xa[:, 0:BRANCH_W]
        return carry

    lax.fori_loop(0, N_CHUNKS, chunk_body, 0)


def _ssd_call(l, direction, xbc, dt, conv_w, conv_b, a_rows, bias_rows, dskip, z=None, yprev=None, norm_g=None):
    bsz = xbc.shape[0]
    final = z is not None
    w = BRANCH_W
    seq = lambda n: pl.BlockSpec((1, T_ALL, n), lambda b: (b, 0, 0))
    lay = lambda r, n: pl.BlockSpec((1, r, n), lambda b: (l, 0, 0))
    in_specs = [seq(M2_XBC), seq(LANES), lay(8, M2_XBC), lay(1, M2_XBC), lay(1, LANES), lay(1, LANES), lay(1, w)]
    args = [xbc, dt, conv_w, conv_b, a_rows, bias_rows, dskip]
    if final:
        in_specs += [seq(w), seq(w), lay(1, w)]
        args += [z, yprev, norm_g]
    return pl.pallas_call(
        functools.partial(_ssd_kernel, direction=direction, final=final),
        grid=(bsz,),
        in_specs=in_specs,
        out_specs=seq(w),
        out_shape=jax.ShapeDtypeStruct((bsz, T_ALL, w), BF16 if final else F32),
        scratch_shapes=[
            pltpu.VMEM((T_ALL + 3 * PAD_ROWS, M2_XBC), F32),
            pltpu.VMEM((T_ALL, M2_XBC), F32),
            pltpu.VMEM((T_ALL, LANES), F32),
            pltpu.VMEM((T_ALL, LANES), F32),
            pltpu.VMEM((M2_HEADS, M2_N, M2_HD), F32),
        ],
        compiler_params=_params(("parallel",)),
        name="ssd_bwd" if final else "ssd_fwd",
    )(*args)


def _merge_kernel(x_ref, mod_ref, g_ref, ya_ref, yb_ref, ys_ref, ym_ref, wg_ref, wb_ref, wo_ref, wr_ref,
                  xo_ref, h2_ref, lg_ref):
    mod = mod_ref[0, 0]
    x = x_ref[0]
    h = (_rms_rows(x) * g_ref[0] * (1.0 + mod[1:2]) + mod[0:1]).astype(BF16)
    acc = jnp.zeros((TOK_TILE, D_MODEL), F32)
    for n, y_ref in enumerate((ya_ref, yb_ref, ys_ref, ym_ref)):
        yv = y_ref[0] if n != 2 else y_ref[...]
        gate = _sigmoid(jnp.dot(h, wg_ref[0, :, n * D_MODEL:(n + 1) * D_MODEL], preferred_element_type=F32))
        acc = acc + gate * jnp.dot(yv, wb_ref[0, n], preferred_element_type=F32)
    xn = x + mod[2:3] * jnp.dot(acc.astype(BF16), wo_ref[0], preferred_element_type=F32)
    xo_ref[0] = xn
    h2 = _rms_rows(xn) * g_ref[1] * (1.0 + mod[4:5]) + mod[3:4]
    h2_ref[0] = h2.astype(BF16)
    lg_ref[0] = lax.dot_general(wr_ref[0], h2, (((1,), (1,)), ((), ())), precision=HI, preferred_element_type=F32)


def _merge_call(l, x, modsel, norm_g, ya, yb, ys, ym, w_gate_in, w_branch, w_out, w_router_t):
    bsz = x.shape[0]
    w = BRANCH_W
    tok = lambda n: pl.BlockSpec((1, TOK_TILE, n), lambda b, i: (b, i, 0))
    return pl.pallas_call(
        _merge_kernel,
        grid=(bsz, N_TOK_TILES),
        in_specs=[
            tok(D_MODEL),
            pl.BlockSpec((1, 1, 8, D_MODEL), lambda b, i: (b, jnp.minimum(i, 1), 0, 0)),
            pl.BlockSpec((2, 1, D_MODEL), lambda b, i: (l, 0, 0)),
            tok(w), tok(w), pl.BlockSpec((TOK_TILE, w), lambda b, i: (i, b)), tok(w),
            pl.BlockSpec((1, D_MODEL, 4 * D_MODEL), lambda b, i: (l, 0, 0)),
            pl.BlockSpec((1, 4, w, D_MODEL), lambda b, i: (l, 0, 0, 0)),
            pl.BlockSpec((1, D_MODEL, D_MODEL), lambda b, i: (l, 0, 0)),
            pl.BlockSpec((1, N_EXPERTS, D_MODEL), lambda b, i: (l, 0, 0)),
        ],
        out_specs=[tok(D_MODEL), tok(D_MODEL), pl.BlockSpec((1, N_EXPERTS, TOK_TILE), lambda b, i: (b, 0, i))],
        out_shape=[jax.ShapeDtypeStruct((bsz, T_ALL, D_MODEL), F32),
                   jax.ShapeDtypeStruct((bsz, T_ALL, D_MODEL), BF16),
                   jax.ShapeDtypeStruct((bsz, N_EXPERTS, T_ALL), F32)],
        compiler_params=_params(("parallel", "parallel")),
        name="merge",
    )(x, modsel, norm_g, ya, yb, ys, ym, w_gate_in, w_branch, w_out, w_router_t)


BISECT_STEPS = 48


def _route_kernel(lg_ref, slot_ref, slot_t_ref, gate_t_ref):
    lg = lg_ref[0]
    sh = lg - jnp.max(lg, axis=0, keepdims=True)
    ex = jnp.exp(sh)
    den = jnp.sum(ex, axis=0, keepdims=True)
    aff = ex / den
    logaff = sh - jnp.log(den)
    ri = lax.broadcasted_iota(jnp.int32, (LANES, LANES), 0)
    ci = lax.broadcasted_iota(jnp.int32, (LANES, LANES), 1)
    upper = jnp.where(ri <= ci, 1.0, 0.0).astype(BF16)

    def prefix_exclusive(mask, lo, hi):
        carry = jnp.zeros((N_EXPERTS, 1), F32)
        parts = []
        for blk in range((hi - lo) // LANES):
            m = mask[:, blk * LANES:(blk + 1) * LANES]
            inc = jnp.dot(m.astype(BF16), upper, preferred_element_type=F32)
            parts.append(inc - m + carry)
            carry = carry + inc[:, LANES - 1:LANES]
        return jnp.concatenate(parts, axis=1)

    segments = ((0, CTX_LEN, CAP_CTX, 0), (CTX_LEN, T_ALL, CAP_LAT, CAP_CTX))

    def bisect(_, bounds):
        out = []
        for (lo, hi, cap, _), (low, high) in zip(segments, bounds):
            mid = 0.5 * (low + high)
            cnt = jnp.sum(jnp.where(logaff[:, lo:hi] >= mid, 1.0, 0.0), axis=1, keepdims=True)
            ok = cnt >= cap
            out.append((jnp.where(ok, mid, low), jnp.where(ok, high, mid)))
        return tuple(out)

    start = tuple((jnp.min(logaff[:, lo:hi], axis=1, keepdims=True), jnp.ones((N_EXPERTS, 1), F32))
                  for lo, hi, _, _ in segments)
    bounds = lax.fori_loop(0, BISECT_STEPS, bisect, start)

    def choose(lo, hi, cap, base, low, high):
        seg = logaff[:, lo:hi]
        gt = jnp.where(seg >= high, 1.0, 0.0)
        eq = jnp.where(seg >= low, 1.0, 0.0) - gt
        need = cap - jnp.sum(gt, axis=1, keepdims=True)
        sel = gt + eq * jnp.where(prefix_exclusive(eq, lo, hi) < need, 1.0, 0.0)
        pos = prefix_exclusive(sel, lo, hi) + base
        slot = jnp.where(sel > 0.0, pos, -1.0)
        gate = sel * aff[:, lo:hi]
        slot_ref[0, :, lo:hi] = slot.astype(jnp.int32)
        fill = jnp.zeros((LANES - N_EXPERTS, LANES), F32)
        for blk in range((hi - lo) // LANES):
            cols = slice(blk * LANES, (blk + 1) * LANES)
            rows = slice(lo + blk * LANES, lo + (blk + 1) * LANES)
            slot_t_ref[0, rows, :] = jnp.transpose(jnp.concatenate([slot[:, cols], fill], axis=0))
            gate_t_ref[0, rows, :] = jnp.transpose(jnp.concatenate([gate[:, cols], fill], axis=0))

    for seg_def, (low, high) in zip(segments, bounds):
        choose(*seg_def, low, high)


def _route_call(logits_t):
    bsz = logits_t.shape[0]
    spec = pl.BlockSpec((1, N_EXPERTS, T_ALL), lambda b: (b, 0, 0))
    spec_t = pl.BlockSpec((1, T_ALL, LANES), lambda b: (b, 0, 0))
    return pl.pallas_call(
        _route_kernel,
        grid=(bsz,),
        in_specs=[spec],
        out_specs=[spec, spec_t, spec_t],
        out_shape=[jax.ShapeDtypeStruct((bsz, N_EXPERTS, T_ALL), jnp.int32),
                   jax.ShapeDtypeStruct((bsz, T_ALL, LANES), F32),
                   jax.ShapeDtypeStruct((bsz, T_ALL, LANES), F32)],
        compiler_params=_params(("parallel",)),
        name="route",
    )(logits_t)


def _one_hot_rows(srow, n_rows, base):
    r = lax.broadcasted_iota(jnp.int32, (n_rows, srow.shape[1]), 0) + base
    return jnp.where(srow == r, 1.0, 0.0).astype(BF16)


def _gather_kernel(slot_ref, h_ref, xs_ref):
    srow = slot_ref[0, pl.ds(pl.program_id(1), 1), :]
    sel_c = _one_hot_rows(srow[:, 0:CTX_LEN], CAP_CTX, 0)
    xs_ref[0, 0:CAP_CTX, :] = jnp.dot(sel_c, h_ref[0, 0:CTX_LEN, :], preferred_element_type=F32).astype(BF16)
    sel_l = _one_hot_rows(srow[:, CTX_LEN:T_ALL], CAP_LAT, CAP_CTX)
    xs_ref[0, CAP_CTX:CAP_ALL, :] = jnp.dot(sel_l, h_ref[0, CTX_LEN:T_ALL, :],
                                            preferred_element_type=F32).astype(BF16)


def _gather_call(slot, h2):
    bsz = slot.shape[0]
    return pl.pallas_call(
        _gather_kernel,
        grid=(bsz, N_EXPERTS),
        in_specs=[pl.BlockSpec((1, N_EXPERTS, T_ALL), lambda b, e: (b, 0, 0)),
                  pl.BlockSpec((1, T_ALL, D_MODEL), lambda b, e: (b, 0, 0))],
        out_specs=pl.BlockSpec((1, CAP_ALL, D_MODEL), lambda b, e: (e, b, 0)),
        out_shape=jax.ShapeDtypeStruct((N_EXPERTS, bsz * CAP_ALL, D_MODEL), BF16),
        compiler_params=_params(("parallel", "arbitrary")),
        name="moe_gather",
    )(slot, h2)


def _ffn_kernel(x_ref, wu_ref, wg_ref, wd_ref, y_ref, acc_scr, *, row_tile):
    f = pl.program_id(1)
    wu = wu_ref[0, 0].astype(BF16)
    wg = wg_ref[0, 0].astype(BF16)
    wd = wd_ref[0, 0].astype(BF16)
    n_rows = x_ref.shape[1]

    @pl.when(f == 0)
    def _():
        acc_scr[...] = jnp.zeros_like(acc_scr)

    for r0 in range(0, n_rows, row_tile):
        x = x_ref[0, r0:r0 + row_tile, :]
        up = jnp.dot(x, wu, preferred_element_type=F32)
        gt = jnp.dot(x, wg, preferred_element_type=F32)
        acc_scr[r0:r0 + row_tile, :] += jnp.dot((_silu(gt) * up).astype(BF16), wd, preferred_element_type=F32)

    @pl.when(f == EXPERT_FF // FF_TILE - 1)
    def _():
        y_ref[0] = acc_scr[...].astype(BF16)


def _ffn_call(l, xs, w_up, w_gate, w_down):
    n_rows = xs.shape[1]
    row_tile = n_rows // 4
    return pl.pallas_call(
        functools.partial(_ffn_kernel, row_tile=row_tile),
        grid=(N_EXPERTS, EXPERT_FF // FF_TILE),
        in_specs=[
            pl.BlockSpec((1, n_rows, D_MODEL), lambda e, f: (e, 0, 0)),
            pl.BlockSpec((1, 1, D_MODEL, FF_TILE), lambda e, f: (l, e, 0, f)),
            pl.BlockSpec((1, 1, D_MODEL, FF_TILE), lambda e, f: (l, e, 0, f)),
            pl.BlockSpec((1, 1, FF_TILE, D_MODEL), lambda e, f: (l, e, f, 0)),
        ],
        out_specs=pl.BlockSpec((1, n_rows, D_MODEL), lambda e, f: (e, 0, 0)),
        out_shape=jax.ShapeDtypeStruct(xs.shape, BF16),
        scratch_shapes=[pltpu.VMEM((n_rows, D_MODEL), F32)],
        compiler_params=_params(("parallel", "arbitrary")),
        name="moe_ffn",
    )(xs, w_up, w_gate, w_down)


def _scatter_kernel(x_ref, y_ref, slot_ref, gate_ref, mod_ref, o_ref):
    slot_t = slot_ref[0]
    gate_t = gate_ref[0]

    def combine(n_slots, base):
        ids = (lax.broadcasted_iota(jnp.int32, (TOK_TILE, n_slots), 1) + base).astype(F32)
        acc = jnp.zeros((TOK_TILE, D_MODEL), F32)
        for e in range(N_EXPERTS):
            sel = jnp.where(slot_t[:, e:e + 1] == ids, gate_t[:, e:e + 1], 0.0).astype(BF16)
            acc = acc + jnp.dot(sel, y_ref[e, base:base + n_slots, :], preferred_element_type=F32)
        o_ref[0] = x_ref[0] + mod_ref[0, 0, 5:6, :] * acc

    pl.when(pl.program_id(1) == 0)(functools.partial(combine, CAP_CTX, 0))
    pl.when(pl.program_id(1) > 0)(functools.partial(combine, CAP_LAT, CAP_CTX))


def _scatter_call(x, y, slot_t, gate_t, modsel):
    bsz = x.shape[0]
    tok = lambda n: pl.BlockSpec((1, TOK_TILE, n), lambda b, i: (b, i, 0))
    return pl.pallas_call(
        _scatter_kernel,
        grid=(bsz, N_TOK_TILES),
        in_specs=[tok(D_MODEL), pl.BlockSpec((N_EXPERTS, CAP_ALL, D_MODEL), lambda b, i: (0, b, 0)),
                  tok(LANES), tok(LANES),
                  pl.BlockSpec((1, 1, 8, D_MODEL), lambda b, i: (b, jnp.minimum(i, 1), 0, 0))],
        out_specs=tok(D_MODEL),
        out_shape=jax.ShapeDtypeStruct(x.shape, F32),
        compiler_params=_params(("parallel", "arbitrary")),
        name="moe_scatter",
    )(x, y, slot_t, gate_t, modsel)


def _rope_tables(hd, n_tile):
    rows = SEQ // GRID_W
    row = jnp.repeat(jnp.arange(rows, dtype=jnp.int32), GRID_W)
    col = jnp.tile(jnp.arange(GRID_W, dtype=jnp.int32), rows)
    half = hd // 2
    freqs = 1.0 / (ROPE_THETA ** (jnp.arange(0, half, 2, dtype=F32) / half))

    def angles(pos):
        ang = pos.astype(F32)[:, None] * freqs[None, :]
        return jnp.concatenate([ang, ang], axis=-1)

    ang = jnp.concatenate([angles(row), angles(col)], axis=-1)
    cos = jnp.concatenate([jnp.ones((CTX_LEN, hd), F32), jnp.cos(ang)], axis=0)
    sin = jnp.concatenate([jnp.zeros((CTX_LEN, hd), F32), jnp.sin(ang)], axis=0)
    return jnp.tile(cos, (1, n_tile)), jnp.tile(sin, (1, n_tile))


def _group_mean_matrix(group):
    idx = np.arange(BRANCH_W) // group
    return jnp.asarray((idx[:, None] == idx[None, :]).astype(np.float32) / group, BF16)


def _s5_tables(a_re, a_im, log_dt, b_re, b_im, c_re, c_im, bsz):
    dt = jnp.exp(log_dt.astype(F32))[..., None]
    ar, ai = a_re.astype(F32), a_im.astype(F32)
    mag = jnp.exp(ar * dt)
    lr, li = mag * jnp.cos(ai * dt), mag * jnp.sin(ai * dt)
    den = ar * ar + ai * ai
    fr = ((lr - 1.0) * ar + li * ai) / den
    fi = (li * ar - (lr - 1.0) * ai) / den
    bbr = fr[..., None] * b_re - fi[..., None] * b_im
    bbi = fr[..., None] * b_im + fi[..., None] * b_re
    eye = jnp.eye(S5_G, dtype=F32)
    bd_in = lambda m: jnp.einsum('ldgph,gk->ldghkp', m, eye).reshape(DEPTH, 2, BRANCH_W, S5_STATE)
    bmat = jnp.concatenate([bd_in(bbr), bd_in(bbi)], axis=-1).astype(BF16)
    bd_out = lambda m: jnp.einsum('ldghp,gk->ldgpkh', m, eye).reshape(DEPTH, 2, S5_STATE, BRANCH_W)
    cmat = jnp.concatenate([bd_out(c_re.astype(F32)), -bd_out(c_im.astype(F32))], axis=-2).astype(BF16)
    lam = jnp.concatenate([lr.reshape(DEPTH, 2, 1, S5_STATE), li.reshape(DEPTH, 2, 1, S5_STATE)], axis=-1)
    lam = jnp.broadcast_to(lam, (DEPTH, 2, bsz, 2 * S5_STATE))
    return bmat, lam, cmat


def _mix_weights(w_in):
    rep = B_HEADS // 2
    cut = lambda lo, n: w_in[:, :, lo:lo + n]
    dup = lambda lo: jnp.concatenate([cut(lo + (h // rep) * B_HD, B_HD) for h in range(B_HEADS)], axis=-1)
    dt_cols = jnp.pad(cut(2304, 2 * M2_HEADS), ((0, 0), (0, 0), (0, LANES - 2 * M2_HEADS)))
    w_mix = jnp.concatenate([cut(0, 1024), dup(1024), dup(1152), cut(1280, 1024), dt_cols], axis=-1)
    return w_mix.astype(BF16), w_in[:, :, 2312:].astype(BF16)


def _prepare(bsz, c, c_ctx, w_ada, b_ada, norm_g, w_in, da_q_g, da_k_g, da_lambda, da_sub_g, gqa_q_g, gqa_k_g,
             s5_a_re, s5_a_im, s5_log_dt, s5_b_re, s5_b_im, s5_c_re, s5_c_im, s5_d, s5_w_glu, s5_b_glu,
             m2_conv_w, m2_conv_b, m2_a_log, m2_dt_bias, m2_d, m2_norm_g, w_branch, w_out, w_router):
    tb = {}
    cc = jnp.concatenate([c, c_ctx[None], jnp.zeros((16 - bsz - 1, D_MODEL), F32)], axis=0)
    mod_all = _ada_call(cc, w_ada, b_ada).reshape(DEPTH, 16, 6, D_MODEL)
    mod_all = jnp.pad(mod_all, ((0, 0), (0, 0), (0, 2), (0, 0)))
    tb["modsel"] = jnp.stack([jnp.broadcast_to(mod_all[:, bsz:bsz + 1], (DEPTH, bsz, 8, D_MODEL)),
                              mod_all[:, :bsz]], axis=2)
    tb["w_mix"], tb["w_gate_in"] = _mix_weights(w_in)
    tb["cosa"], tb["sina"] = _rope_tables(A_HD, BRANCH_W // A_HD)
    tb["cosb"], tb["sinb"] = _rope_tables(B_HD, BRANCH_W // B_HD)
    tb["gmat_a"], tb["gmat_b"] = _group_mean_matrix(A_HD), _group_mean_matrix(B_HD)
    tile = lambda g: jnp.tile(g.astype(F32), (1, BRANCH_W // g.shape[-1]))
    gains = jnp.stack([tile(da_q_g), tile(da_k_g), tile(gqa_q_g), tile(gqa_k_g)], axis=1)
    tb["gains"] = jnp.pad(gains, ((0, 0), (0, 4), (0, 0)))
    amax = lambda g: jnp.max(jnp.abs(g.astype(F32)), axis=-1)
    tb["logit_bounds"] = 1.05 * jnp.stack([math.sqrt(A_HD) * amax(da_q_g) * amax(da_k_g),
                                           math.sqrt(B_HD) * amax(gqa_q_g) * amax(gqa_k_g)], axis=1).reshape(-1)
    lf = da_lambda.astype(F32)
    tb["lam_init"] = [0.8 - 0.6 * math.exp(-0.3 * l) for l in range(DEPTH)]
    tb["lam"] = (jnp.exp(jnp.sum(lf[:, 0] * lf[:, 1], axis=-1)) - jnp.exp(jnp.sum(lf[:, 2] * lf[:, 3], axis=-1))
                 + jnp.asarray(tb["lam_init"], F32)).reshape(DEPTH, 1, 1)
    row3 = lambda t: t.astype(F32).reshape(DEPTH, 1, t.shape[-1])
    tb["sub_g"] = row3(tile(da_sub_g))
    tb["s5_bmat"], tb["s5_lam"], tb["s5_cmat"] = _s5_tables(
        s5_a_re, s5_a_im, s5_log_dt, s5_b_re, s5_b_im, s5_c_re, s5_c_im, bsz)
    tb["s5_d"], tb["s5_b_glu"], tb["s5_w_glu"] = row3(s5_d), row3(s5_b_glu), s5_w_glu.astype(BF16)
    lane_row = lambda t: jnp.pad(t.astype(F32).reshape(DEPTH, 1, 2 * M2_HEADS),
                                 ((0, 0), (0, 0), (0, LANES - 2 * M2_HEADS)))
    tb["m2_a"] = lane_row(-jnp.exp(m2_a_log.astype(F32)))
    tb["m2_bias"] = lane_row(m2_dt_bias)
    tb["m2_dskip"] = row3(jnp.repeat(m2_d.astype(F32), M2_HD, axis=-1))
    tb["conv_w"] = jnp.pad(m2_conv_w.astype(F32), ((0, 0), (0, 8 - M2_CONV), (0, 0)))
    tb["conv_b"] = row3(m2_conv_b)
    tb["m2_norm_g"] = row3(m2_norm_g)
    tb["norm_rows"] = norm_g.astype(F32).reshape(DEPTH * 2, 1, D_MODEL)
    tb["w_branch"], tb["w_out"] = w_branch.astype(BF16), w_out.astype(BF16)
    tb["w_router_t"] = jnp.swapaxes(w_router.astype(F32), 1, 2)
    return tb


def _mixers(l, xs, tb):
    bsz = xs.shape[0]
    qa, ka, va, qb, kb, vb, u_tb, z, xbc, dt = _inproj_call(
        l, xs, tb["modsel"][l], tb["norm_rows"], tb["w_mix"], tb["gains"], tb["gmat_a"], tb["gmat_b"],
        tb["cosa"], tb["sina"], tb["cosb"], tb["sinb"])
    ya = _attn_call(l, tb["logit_bounds"], qa, ka, va, tb["lam"], tb["sub_g"], tb["gmat_b"], 2,
                    1.0 - tb["lam_init"][l], "diff_attn")
    yb = _attn_call(l, tb["logit_bounds"], qb, kb, vb, tb["lam"], tb["sub_g"], tb["gmat_b"], 1, None, "gqa_attn")
    u_rows = u_tb.reshape(T_ALL * bsz, BRANCH_W)
    y_s5 = _s5_call(u_rows, tb["s5_bmat"][l], tb["s5_lam"][l], tb["s5_cmat"][l], bsz)
    ys = _s5_glu_call(l, u_rows, y_s5, tb["s5_d"], tb["s5_w_glu"], tb["s5_b_glu"], bsz)
    ys = ys.reshape(T_ALL, bsz * BRANCH_W)
    ssd_args = (xbc, dt, tb["conv_w"], tb["conv_b"], tb["m2_a"], tb["m2_bias"], tb["m2_dskip"])
    y_fwd = _ssd_call(l, 0, *ssd_args)
    ym = _ssd_call(l, 1, *ssd_args, z=z, yprev=y_fwd, norm_g=tb["m2_norm_g"])
    return ya, yb, ys, ym


def _layer(l, xs, tb, w_up, w_gate, w_down):
    modsel = tb["modsel"][l]
    ya, yb, ys, ym = _mixers(l, xs, tb)
    x_mid, h2, logits_t = _merge_call(l, xs, modsel, tb["norm_rows"], ya, yb, ys, ym,
                                      tb["w_gate_in"], tb["w_branch"], tb["w_out"], tb["w_router_t"])
    slot, slot_t, gate_t = _route_call(logits_t)
    xe = _gather_call(slot, h2)
    ye = _ffn_call(l, xe, w_up, w_gate, w_down)
    return _scatter_call(x_mid, ye, slot_t, gate_t, modsel)


def kernel(x, c, ctx, c_ctx, w_ada, b_ada, norm_g, w_in, da_q_g, da_k_g, da_lambda, da_sub_g, gqa_q_g, gqa_k_g,
           s5_a_re, s5_a_im, s5_log_dt, s5_b_re, s5_b_im, s5_c_re, s5_c_im, s5_d, s5_w_glu, s5_b_glu,
           m2_conv_w, m2_conv_b, m2_a_log, m2_dt_bias, m2_d, m2_norm_g, w_branch, w_out, w_router,
           w_up, w_gate, w_down):
    bsz = x.shape[0]
    assert x.shape == (bsz, SEQ, D_MODEL) and ctx.shape == (bsz, CTX_LEN, D_MODEL) and bsz == 8
    tb = _prepare(bsz, c, c_ctx, w_ada, b_ada, norm_g, w_in, da_q_g, da_k_g, da_lambda, da_sub_g, gqa_q_g,
                  gqa_k_g, s5_a_re, s5_a_im, s5_log_dt, s5_b_re, s5_b_im, s5_c_re, s5_c_im, s5_d, s5_w_glu,
                  s5_b_glu, m2_conv_w, m2_conv_b, m2_a_log, m2_dt_bias, m2_d, m2_norm_g, w_branch, w_out,
                  w_router)
    xs = jnp.concatenate([ctx, x], axis=1)
    for l in range(DEPTH):
        xs = _layer(l, xs, tb, w_up, w_gate, w_down)
    return xs[:, CTX_LEN:, :]
```

```python
import functools
import math

import jax
import jax.numpy as jnp
import numpy as np
from jax import lax
from jax.experimental import pallas as pl
from jax.experimental.pallas import tpu as pltpu

F32 = jnp.float32
BF16 = jnp.bfloat16
HI = lax.Precision.HIGHEST

D_MODEL = 1024
SEQ = 2048
DEPTH = 4
GRID_W = 64
CTX_LEN = 256
T_ALL = CTX_LEN + SEQ
BRANCH_W = 256
ROPE_THETA = 10000.0
EPS = 1e-6
A_HD = 32
A_VD = 64
A_HEADS = 4
B_HEADS = 4
B_HD = 64
S5_H = 16
S5_G = 16
S5_P = 64
S5_STATE = S5_G * S5_P
M2_HD = 64
M2_HEADS = 4
M2_GROUPS = 2
M2_N = 64
M2_CONV = 5
M2_XBC = 512
N_EXPERTS = 16
EXPERT_FF = 2 * D_MODEL
EC_FACTOR = 2
CAP_CTX = EC_FACTOR * CTX_LEN // N_EXPERTS
CAP_LAT = EC_FACTOR * SEQ // N_EXPERTS
CAP_ALL = CAP_CTX + CAP_LAT

LANES = 128
TOK_TILE = 256
N_TOK_TILES = T_ALL // TOK_TILE
CHUNK = 128
N_CHUNKS = T_ALL // CHUNK
CTX_CHUNKS = CTX_LEN // CHUNK
MIX_COLS = 2944
FF_TILE = 512
VMEM_LIMIT = 56 * 1024 * 1024
LOG2E = 1.0 / math.log(2.0)
SAFE_LOGIT = 40.0


def _params(sem, vmem=VMEM_LIMIT):
    return pltpu.CompilerParams(dimension_semantics=sem, vmem_limit_bytes=vmem)


def _sigmoid(x):
    return 1.0 / (1.0 + jnp.exp(-x))


def _silu(x):
    return x * _sigmoid(x)


def _rms_rows(x):
    return x * lax.rsqrt(jnp.mean(x * x, axis=-1, keepdims=True) + EPS)


def _ada_kernel(c_ref, w_ref, b_ref, o_ref):
    sc = _silu(c_ref[...])
    o_ref[0] = jnp.dot(sc, w_ref[0], precision=HI, preferred_element_type=F32) + b_ref[0]


def _ada_call(cc, w_ada, b_ada):
    tn = 1536
    return pl.pallas_call(
        _ada_kernel,
        grid=(DEPTH, 6 * D_MODEL // tn),
        in_specs=[
            pl.BlockSpec((16, D_MODEL), lambda l, j: (0, 0)),
            pl.BlockSpec((1, D_MODEL, tn), lambda l, j: (l, 0, j)),
            pl.BlockSpec((1, 1, tn), lambda l, j: (l, 0, j)),
        ],
        out_specs=pl.BlockSpec((1, 16, tn), lambda l, j: (l, 0, j)),
        out_shape=jax.ShapeDtypeStruct((DEPTH, 16, 6 * D_MODEL), F32),
        compiler_params=_params(("arbitrary", "arbitrary")),
        name="ada_mod",
    )(cc, w_ada, b_ada.reshape(DEPTH, 1, 6 * D_MODEL))


def _group_mean(x, gmat):
    hi = x.astype(BF16)
    lo = (x - hi.astype(F32)).astype(BF16)
    return jnp.dot(hi, gmat, preferred_element_type=F32) + jnp.dot(lo, gmat, preferred_element_type=F32)


def _group_norm_rope(x, gmat, gain, cos, sin, shift):
    xn = x * lax.rsqrt(_group_mean(x * x, gmat) + EPS) * gain
    lane = lax.broadcasted_iota(jnp.int32, xn.shape, 1)
    up = pltpu.roll(xn, BRANCH_W - shift, 1)
    dn = pltpu.roll(xn, shift, 1)
    rot = jnp.where((lane % (2 * shift)) < shift, -up, dn)
    return xn * cos + rot * sin


def _inproj_kernel(x_ref, mod_ref, g_ref, w_ref, gains_ref, ga_ref, gb_ref,
                   cosa_ref, sina_ref, cosb_ref, sinb_ref,
                   qa_ref, ka_ref, va_ref, qb_ref, kb_ref, vb_ref, u_ref, z_ref, xbc_ref, dt_ref):
    mod = mod_ref[0, 0]
    h = _rms_rows(x_ref[0]) * g_ref[0] * (1.0 + mod[1:2]) + mod[0:1]
    p = jnp.dot(h.astype(BF16), w_ref[0], preferred_element_type=F32)
    gains = gains_ref[0]
    w = BRANCH_W
    qa = _group_norm_rope(p[:, 0:w], ga_ref[...], gains[0:1], cosa_ref[...], sina_ref[...], A_HD // 4)
    ka = _group_norm_rope(p[:, w:2 * w], ga_ref[...], gains[1:2], cosa_ref[...], sina_ref[...], A_HD // 4)
    qb = _group_norm_rope(p[:, 3 * w:4 * w], gb_ref[...], gains[2:3], cosb_ref[...], sinb_ref[...], B_HD // 4)
    kb = _group_norm_rope(p[:, 4 * w:5 * w], gb_ref[...], gains[3:4], cosb_ref[...], sinb_ref[...], B_HD // 4)
    qa_ref[0] = (qa * (LOG2E / math.sqrt(A_HD))).astype(BF16)
    ka_ref[0] = ka.astype(BF16)
    va_ref[0] = p[:, 2 * w:3 * w].astype(BF16)
    qb_ref[0] = (qb * (LOG2E / math.sqrt(B_HD))).astype(BF16)
    kb_ref[0] = kb.astype(BF16)
    vb_ref[0] = p[:, 5 * w:6 * w].astype(BF16)
    u_ref[...] = p[:, 6 * w:7 * w].astype(BF16)
    z_ref[0] = p[:, 7 * w:8 * w]
    xbc_ref[0] = p[:, 8 * w:10 * w]
    dt_ref[0] = p[:, 10 * w:10 * w + LANES]


def _inproj_call(l, x, modsel, norm_g, w_mix, gains, gmat_a, gmat_b, cosa, sina, cosb, sinb):
    bsz = x.shape[0]
    w = BRANCH_W
    tok = lambda n: pl.BlockSpec((1, TOK_TILE, n), lambda b, i: (b, i, 0))
    tab = pl.BlockSpec((TOK_TILE, w), lambda b, i: (i, 0))
    full2 = lambda a: pl.BlockSpec(a.shape, lambda b, i: (0, 0))
    out_bt = lambda n, dt: jax.ShapeDtypeStruct((bsz, T_ALL, n), dt)
    return pl.pallas_call(
        _inproj_kernel,
        grid=(bsz, N_TOK_TILES),
        in_specs=[
            tok(D_MODEL),
            pl.BlockSpec((1, 1, 8, D_MODEL), lambda b, i: (b, jnp.minimum(i, 1), 0, 0)),
            pl.BlockSpec((1, 1, D_MODEL), lambda b, i: (2 * l, 0, 0)),
            pl.BlockSpec((1, D_MODEL, MIX_COLS), lambda b, i: (l, 0, 0)),
            pl.BlockSpec((1, 8, w), lambda b, i: (l, 0, 0)),
            full2(gmat_a), full2(gmat_b), tab, tab, tab, tab,
        ],
        out_specs=[tok(w), tok(w), tok(w), tok(w), tok(w), tok(w),
                   pl.BlockSpec((TOK_TILE, w), lambda b, i: (i, b)),
                   tok(w), tok(M2_XBC), tok(LANES)],
        out_shape=[out_bt(w, BF16)] * 6 + [jax.ShapeDtypeStruct((T_ALL, bsz * w), BF16),
                                            out_bt(w, F32), out_bt(M2_XBC, F32), out_bt(LANES, F32)],
        compiler_params=_params(("parallel", "parallel")),
        name="in_proj",
    )(x, modsel, norm_g, w_mix, gains, gmat_a, gmat_b, cosa, sina, cosb, sinb)


def _attn_kernel(bound_ref, q_ref, k_ref, v_ref, lam_ref, subg_ref, gmat_ref, o_ref, *,
                 n_maps, post_scale, bound_idx):
    lane = lax.broadcasted_iota(jnp.int32, (TOK_TILE, BRANCH_W), 1)
    q = q_ref[0]
    map_w = A_VD // n_maps

    def attend(n_keys, shift):
        k = k_ref[0, 0:n_keys, :]
        v = v_ref[0, 0:n_keys, :]
        out = jnp.zeros((TOK_TILE, BRANCH_W), F32)
        for head in range(A_HEADS):
            acc = None
            for m in range(n_maps):
                lo = head * A_VD + m * map_w
                qm = jnp.where((lane >= lo) & (lane < lo + map_w), q, jnp.zeros_like(q))
                s = lax.dot_general(qm, k, (((1,), (1,)), ((), ())), preferred_element_type=F32)
                if shift:
                    s = s - jnp.max(s, axis=-1, keepdims=True)
                e = jnp.exp2(s)
                r = 1.0 / jnp.sum(e, axis=-1, keepdims=True)
                o = jnp.dot(e.astype(BF16), v, preferred_element_type=F32)
                acc = o * r if m == 0 else acc - o * (r * lam_ref[0])
            out = jnp.where((lane >= head * A_VD) & (lane < (head + 1) * A_VD), acc, out)
        if post_scale is not None:
            out = out * lax.rsqrt(_group_mean(out * out, gmat_ref[...]) + EPS) * subg_ref[0] * post_scale
        o_ref[0] = out.astype(BF16)

    is_ctx = pl.program_id(1) == 0
    small = bound_ref[bound_idx] <= SAFE_LOGIT
    for n_keys, tile_sel in ((CTX_LEN, is_ctx), (T_ALL, jnp.logical_not(is_ctx))):
        for shift, bound_sel in ((False, small), (True, jnp.logical_not(small))):
            pl.when(jnp.logical_and(tile_sel, bound_sel))(functools.partial(attend, n_keys, shift))


def _attn_call(l, bounds, q, k, v, lam, subg, gmat, n_maps, post_scale, name):
    bsz = q.shape[0]
    w = BRANCH_W
    return pl.pallas_call(
        functools.partial(_attn_kernel, n_maps=n_maps, post_scale=post_scale,
                          bound_idx=2 * l + (0 if n_maps == 2 else 1)),
        grid=(bsz, N_TOK_TILES),
        in_specs=[
            pl.BlockSpec(memory_space=pltpu.SMEM),
            pl.BlockSpec((1, TOK_TILE, w), lambda b, i: (b, i, 0)),
            pl.BlockSpec((1, T_ALL, w), lambda b, i: (b, 0, 0)),
            pl.BlockSpec((1, T_ALL, w), lambda b, i: (b, 0, 0)),
            pl.BlockSpec((1, 1, 1), lambda b, i: (l, 0, 0)),
            pl.BlockSpec((1, 1, w), lambda b, i: (l, 0, 0)),
            pl.BlockSpec(gmat.shape, lambda b, i: (0, 0)),
        ],
        out_specs=pl.BlockSpec((1, TOK_TILE, w), lambda b, i: (b, i, 0)),
        out_shape=jax.ShapeDtypeStruct((bsz, T_ALL, w), BF16),
        compiler_params=_params(("parallel", "parallel")),
        name=name,
    )(bounds, q, k, v, lam, subg, gmat)


def _scan_chunk_index(direction, j):
    back = jnp.where(j < CTX_CHUNKS, CTX_CHUNKS - 1 - j, N_CHUNKS + CTX_CHUNKS - 1 - j)
    return jnp.where(direction == 0, j, back)


def _s5_kernel(u_ref, bmat_ref, lam_ref, cmat_ref, y_ref, bu_scr, h_scr, *, bsz):
    direction = pl.program_id(0)

    @pl.when(pl.program_id(1) == 0)
    def _():
        h_scr[...] = jnp.zeros_like(h_scr)

    bu_scr[...] = jnp.dot(u_ref[...], bmat_ref[0], preferred_element_type=F32)
    lam_re = lam_ref[0, :, 0:S5_STATE]
    lam_im = lam_ref[0, :, S5_STATE:2 * S5_STATE]

    def step(t, carry):
        h_re, h_im = carry
        tt = jnp.where(direction == 0, t, CHUNK - 1 - t)
        rows = pl.ds(pl.multiple_of(tt * bsz, bsz), bsz)
        n_re = lam_re * h_re - lam_im * h_im + bu_scr[rows, 0:S5_STATE]
        n_im = lam_re * h_im + lam_im * h_re + bu_scr[rows, S5_STATE:2 * S5_STATE]
        bu_scr[rows, 0:S5_STATE] = n_re
        bu_scr[rows, S5_STATE:2 * S5_STATE] = n_im
        return n_re, n_im

    h_re, h_im = lax.fori_loop(0, CHUNK, step, (h_scr[:, 0:S5_STATE], h_scr[:, S5_STATE:2 * S5_STATE]),
                               unroll=4)
    h_scr[:, 0:S5_STATE] = h_re
    h_scr[:, S5_STATE:2 * S5_STATE] = h_im
    y_ref[0] = jnp.dot(bu_scr[...].astype(BF16), cmat_ref[0], preferred_element_type=F32)


def _s5_call(u_tb, bmat, lam, cmat, bsz):
    rows = CHUNK * bsz
    return pl.pallas_call(
        functools.partial(_s5_kernel, bsz=bsz),
        grid=(2, N_CHUNKS),
        in_specs=[
            pl.BlockSpec((rows, BRANCH_W), lambda d, j: (_scan_chunk_index(d, j), 0)),
            pl.BlockSpec((1, BRANCH_W, 2 * S5_STATE), lambda d, j: (d, 0, 0)),
            pl.BlockSpec((1, bsz, 2 * S5_STATE), lambda d, j: (d, 0, 0)),
            pl.BlockSpec((1, 2 * S5_STATE, BRANCH_W), lambda d, j: (d, 0, 0)),
        ],
        out_specs=pl.BlockSpec((1, rows, BRANCH_W), lambda d, j: (d, _scan_chunk_index(d, j), 0)),
        out_shape=jax.ShapeDtypeStruct((2, T_ALL * bsz, BRANCH_W), F32),
        scratch_shapes=[pltpu.VMEM((rows, 2 * S5_STATE), F32), pltpu.VMEM((bsz, 2 * S5_STATE), F32)],
        compiler_params=_params(("arbitrary", "arbitrary")),
        name="s5_scan",
    )(u_tb, bmat, lam, cmat)


def _s5_glu_kernel(u_ref, y_ref, d_ref, w_ref, b_ref, o_ref):
    t = d_ref[0] * u_ref[...].astype(F32) + y_ref[0] + y_ref[1]
    t = 0.5 * t * (1.0 + jnp.tanh(math.sqrt(2.0 / math.pi) * (t + 0.044715 * (t * t * t))))
    gate = jnp.dot(t.astype(BF16), w_ref[0], preferred_element_type=F32) + b_ref[0]
    o_ref[...] = (t * _sigmoid(gate)).astype(BF16)


def _s5_glu_call(l, u_tb, y, s5_d, w_glu, b_glu, bsz):
    rows = CHUNK * bsz
    w = BRANCH_W
    return pl.pallas_call(
        _s5_glu_kernel,
        grid=(N_CHUNKS,),
        in_specs=[
            pl.BlockSpec((rows, w), lambda j: (j, 0)),
            pl.BlockSpec((2, rows, w), lambda j: (0, j, 0)),
            pl.BlockSpec((1, 1, w), lambda j: (l, 0, 0)),
            pl.BlockSpec((1, w, w), lambda j: (l, 0, 0)),
            pl.BlockSpec((1, 1, w), lambda j: (l, 0, 0)),
        ],
        out_specs=pl.BlockSpec((rows, w), lambda j: (j, 0)),
        out_shape=jax.ShapeDtypeStruct((T_ALL * bsz, w), BF16),
        compiler_params=_params(("parallel",)),
        name="s5_glu",
    )(u_tb, y, s5_d, w_glu, b_glu)


PAD_ROWS = 8


def _softplus(x):
    return jnp.maximum(x, 0.0) + jnp.log(1.0 + jnp.exp(-jnp.abs(x)))


def _ssd_kernel(*refs, direction, final):
    if final:
        (xbc_ref, dt_ref, cw_ref, cb_ref, a_ref, bias_ref, dskip_ref, z_ref, yprev_ref, ng_ref,
         o_ref, pad_scr, xact_scr, dtv_scr, ad_scr, st_scr) = refs
    else:
        (xbc_ref, dt_ref, cw_ref, cb_ref, a_ref, bias_ref, dskip_ref,
         o_ref, pad_scr, xact_scr, dtv_scr, ad_scr, st_scr) = refs
    rev = direction == 1

    lat0 = CTX_LEN + 3 * PAD_ROWS - PAD_ROWS
    pad_scr[...] = jnp.zeros_like(pad_scr)
    pad_scr[PAD_ROWS:PAD_ROWS + CTX_LEN, :] = xbc_ref[0, 0:CTX_LEN, :]
    pad_scr[lat0:lat0 + SEQ, :] = xbc_ref[0, CTX_LEN:T_ALL, :]
    for c in range(N_CHUNKS):
        base = c * CHUNK + (PAD_ROWS if c < CTX_CHUNKS else lat0 - CTX_LEN)
        acc = jnp.broadcast_to(cb_ref[0], (CHUNK, M2_XBC))
        for tap in range(M2_CONV):
            start = base + tap - M2_CONV // 2
            acc = acc + cw_ref[0, tap:tap + 1, :] * pad_scr[start:start + CHUNK, :]
        xact_scr[c * CHUNK:(c + 1) * CHUNK, :] = _silu(acc)

    dtv = _softplus(dt_ref[0] + bias_ref[0])
    dtv_scr[...] = dtv
    ad_scr[...] = dtv * a_ref[0]
    st_scr[...] = jnp.zeros_like(st_scr)

    ti = lax.broadcasted_iota(jnp.int32, (CHUNK, CHUNK), 0)
    si = lax.broadcasted_iota(jnp.int32, (CHUNK, CHUNK), 1)
    tri = jnp.where(ti >= si, 1.0, 0.0).astype(F32)
    keep = (si >= ti) if rev else (ti >= si)

    def chunk_body(j, carry):
        ci = _scan_chunk_index(direction, j)
        rows = pl.ds(pl.multiple_of(ci * CHUNK, CHUNK), CHUNK)
        xa = xact_scr[rows, :]
        dtc = dtv_scr[rows, :]
        adc = ad_scr[rows, :]
        cs = jnp.dot(tri, adc, precision=HI, preferred_element_type=F32)
        tot = cs[CHUNK - 1:CHUNK, :]
        pcs = cs - adc if rev else cs
        pcs_t = jnp.transpose(pcs)
        ys = []
        for grp in range(M2_GROUPS):
            b0 = BRANCH_W + grp * M2_N
            c0 = BRANCH_W + M2_GROUPS * M2_N + grp * M2_N
            bg = xa[:, b0:b0 + M2_N]
            cg = xa[:, c0:c0 + M2_N].astype(BF16)
            gmat = lax.dot_general(cg, bg.astype(BF16), (((1,), (1,)), ((), ())), preferred_element_type=F32)
            for hh in range(M2_HEADS // M2_GROUPS):
                head = grp * (M2_HEADS // M2_GROUPS) + hh
                ch = direction * M2_HEADS + head
                col = pcs[:, ch:ch + 1]
                row = pcs_t[ch:ch + 1, :]
                tot_h = tot[:, ch:ch + 1]
                arg = (row - col) if rev else (col - row)
                lmat = jnp.exp(jnp.where(keep, arg, -jnp.inf))
                xd = (xa[:, head * M2_HD:(head + 1) * M2_HD] * dtc[:, ch:ch + 1]).astype(BF16)
                y_diag = jnp.dot((gmat * lmat).astype(BF16), xd, preferred_element_type=F32)
                off = jnp.exp(tot_h - col) if rev else jnp.exp(col)
                dte = jnp.exp(col) if rev else jnp.exp(tot_h - col)
                state = st_scr[head]
                y_off = jnp.dot(cg, state.astype(BF16), preferred_element_type=F32) * off
                st_scr[head] = jnp.exp(tot_h) * state + lax.dot_general(
                    (bg * dte).astype(BF16), xd, (((0,), (0,)), ((), ())), preferred_element_type=F32)
                ys.append(y_diag + y_off)
        y = jnp.concatenate(ys, axis=-1)
        if final:
            y = y + yprev_ref[0, rows, :]
            g = y * _silu(z_ref[0, rows, :])
            o_ref[0, rows, :] = (_rms_rows(g) * ng_ref[0]).astype(BF16)
        else:
            o_ref[0, rows, :] = y + dskip_ref[0] * xa[:, 0:BRANCH_W]
        return carry

    lax.fori_loop(0, N_CHUNKS, chunk_body, 0)


def _ssd_call(l, direction, xbc, dt, conv_w, conv_b, a_rows, bias_rows, dskip, z=None, yprev=None, norm_g=None):
    bsz = xbc.shape[0]
    final = z is not None
    w = BRANCH_W
    seq = lambda n: pl.BlockSpec((1, T_ALL, n), lambda b: (b, 0, 0))
    lay = lambda r, n: pl.BlockSpec((1, r, n), lambda b: (l, 0, 0))
    in_specs = [seq(M2_XBC), seq(LANES), lay(8, M2_XBC), lay(1, M2_XBC), lay(1, LANES), lay(1, LANES), lay(1, w)]
    args = [xbc, dt, conv_w, conv_b, a_rows, bias_rows, dskip]
    if final:
        in_specs += [seq(w), seq(w), lay(1, w)]
        args += [z, yprev, norm_g]
    return pl.pallas_call(
        functools.partial(_ssd_kernel, direction=direction, final=final),
        grid=(bsz,),
        in_specs=in_specs,
        out_specs=seq(w),
        out_shape=jax.ShapeDtypeStruct((bsz, T_ALL, w), BF16 if final else F32),
        scratch_shapes=[
            pltpu.VMEM((T_ALL + 3 * PAD_ROWS, M2_XBC), F32),
            pltpu.VMEM((T_ALL, M2_XBC), F32),
            pltpu.VMEM((T_ALL, LANES), F32),
            pltpu.VMEM((T_ALL, LANES), F32),
            pltpu.VMEM((M2_HEADS, M2_N, M2_HD), F32),
        ],
        compiler_params=_params(("parallel",)),
        name="ssd_bwd" if final else "ssd_fwd",
    )(*args)


MERGE_TILE = 768


def _merge_kernel(x_ref, mod_ref, g_ref, ya_ref, yb_ref, ys_ref, ym_ref, wg_ref, wb_ref, wo_ref, wr_ref,
                  xo_ref, h2_ref, lg_ref):
    row = pl.program_id(1) * MERGE_TILE + lax.broadcasted_iota(jnp.int32, (MERGE_TILE, 1), 0)
    is_ctx = row < CTX_LEN
    mod = lambda k: jnp.where(is_ctx, mod_ref[0, 0, k:k + 1, :], mod_ref[0, 1, k:k + 1, :])
    x = x_ref[0]
    h = (_rms_rows(x) * g_ref[0] * (1.0 + mod(1)) + mod(0)).astype(BF16)
    acc = jnp.zeros((MERGE_TILE, D_MODEL), F32)
    for n, y_ref in enumerate((ya_ref, yb_ref, ys_ref, ym_ref)):
        yv = y_ref[0] if n != 2 else y_ref[...]
        gate = _sigmoid(jnp.dot(h, wg_ref[0, :, n * D_MODEL:(n + 1) * D_MODEL], preferred_element_type=F32))
        acc = acc + gate * jnp.dot(yv, wb_ref[0, n], preferred_element_type=F32)
    xn = x + mod(2) * jnp.dot(acc.astype(BF16), wo_ref[0], preferred_element_type=F32)
    xo_ref[0] = xn
    h2 = _rms_rows(xn) * g_ref[1] * (1.0 + mod(4)) + mod(3)
    h2_ref[0] = h2.astype(BF16)
    lg_ref[0] = lax.dot_general(wr_ref[0], h2, (((1,), (1,)), ((), ())), precision=HI, preferred_element_type=F32)


def _merge_call(l, x, modsel, norm_g, ya, yb, ys, ym, w_gate_in, w_branch, w_out, w_router_t):
    bsz = x.shape[0]
    w = BRANCH_W
    tok = lambda n: pl.BlockSpec((1, MERGE_TILE, n), lambda b, i: (b, i, 0))
    return pl.pallas_call(
        _merge_kernel,
        grid=(bsz, T_ALL // MERGE_TILE),
        in_specs=[
            tok(D_MODEL),
            pl.BlockSpec((1, 2, 8, D_MODEL), lambda b, i: (b, 0, 0, 0)),
            pl.BlockSpec((2, 1, D_MODEL), lambda b, i: (l, 0, 0)),
            tok(w), tok(w), pl.BlockSpec((MERGE_TILE, w), lambda b, i: (i, b)), tok(w),
            pl.BlockSpec((1, D_MODEL, 4 * D_MODEL), lambda b, i: (l, 0, 0)),
            pl.BlockSpec((1, 4, w, D_MODEL), lambda b, i: (l, 0, 0, 0)),
            pl.BlockSpec((1, D_MODEL, D_MODEL), lambda b, i: (l, 0, 0)),
            pl.BlockSpec((1, N_EXPERTS, D_MODEL), lambda b, i: (l, 0, 0)),
        ],
        out_specs=[tok(D_MODEL), tok(D_MODEL), pl.BlockSpec((1, N_EXPERTS, MERGE_TILE), lambda b, i: (b, 0, i))],
        out_shape=[jax.ShapeDtypeStruct((bsz, T_ALL, D_MODEL), F32),
                   jax.ShapeDtypeStruct((bsz, T_ALL, D_MODEL), BF16),
                   jax.ShapeDtypeStruct((bsz, N_EXPERTS, T_ALL), F32)],
        compiler_params=_params(("parallel", "parallel")),
        name="merge",
    )(x, modsel, norm_g, ya, yb, ys, ym, w_gate_in, w_branch, w_out, w_router_t)


BISECT_STEPS = 48


def _route_kernel(lg_ref, slot_ref, slot_t_ref, gate_t_ref):
    lg = lg_ref[0]
    sh = lg - jnp.max(lg, axis=0, keepdims=True)
    ex = jnp.exp(sh)
    den = jnp.sum(ex, axis=0, keepdims=True)
    aff = ex / den
    logaff = sh - jnp.log(den)
    ri = lax.broadcasted_iota(jnp.int32, (LANES, LANES), 0)
    ci = lax.broadcasted_iota(jnp.int32, (LANES, LANES), 1)
    upper = jnp.where(ri <= ci, 1.0, 0.0).astype(BF16)

    def prefix_exclusive(mask, lo, hi):
        carry = jnp.zeros((N_EXPERTS, 1), F32)
        parts = []
        for blk in range((hi - lo) // LANES):
            m = mask[:, blk * LANES:(blk + 1) * LANES]
            inc = jnp.dot(m.astype(BF16), upper, preferred_element_type=F32)
            parts.append(inc - m + carry)
            carry = carry + inc[:, LANES - 1:LANES]
        return jnp.concatenate(parts, axis=1)

    segments = ((0, CTX_LEN, CAP_CTX, 0), (CTX_LEN, T_ALL, CAP_LAT, CAP_CTX))

    def bisect(_, bounds):
        out = []
        for (lo, hi, cap, _), (low, high) in zip(segments, bounds):
            mid = 0.5 * (low + high)
            cnt = jnp.sum(jnp.where(logaff[:, lo:hi] >= mid, 1.0, 0.0), axis=1, keepdims=True)
            ok = cnt >= cap
            out.append((jnp.where(ok, mid, low), jnp.where(ok, high, mid)))
        return tuple(out)

    start = tuple((jnp.min(logaff[:, lo:hi], axis=1, keepdims=True), jnp.ones((N_EXPERTS, 1), F32))
                  for lo, hi, _, _ in segments)
    bounds = lax.fori_loop(0, BISECT_STEPS, bisect, start)

    def choose(lo, hi, cap, base, low, high):
        seg = logaff[:, lo:hi]
        gt = jnp.where(seg >= high, 1.0, 0.0)
        eq = jnp.where(seg >= low, 1.0, 0.0) - gt
        need = cap - jnp.sum(gt, axis=1, keepdims=True)
        sel = gt + eq * jnp.where(prefix_exclusive(eq, lo, hi) < need, 1.0, 0.0)
        pos = prefix_exclusive(sel, lo, hi) + base
        slot = jnp.where(sel > 0.0, pos, -1.0)
        gate = sel * aff[:, lo:hi]
        slot_ref[0, :, lo:hi] = slot.astype(jnp.int32)
        fill = jnp.zeros((LANES - N_EXPERTS, LANES), F32)
        for blk in range((hi - lo) // LANES):
            cols = slice(blk * LANES, (blk + 1) * LANES)
            rows = slice(lo + blk * LANES, lo + (blk + 1) * LANES)
            slot_t_ref[0, rows, :] = jnp.transpose(jnp.concatenate([slot[:, cols], fill], axis=0))
            gate_t_ref[0, rows, :] = jnp.transpose(jnp.concatenate([gate[:, cols], fill], axis=0))

    for seg_def, (low, high) in zip(segments, bounds):
        choose(*seg_def, low, high)


def _route_call(logits_t):
    bsz = logits_t.shape[0]
    spec = pl.BlockSpec((1, N_EXPERTS, T_ALL), lambda b: (b, 0, 0))
    spec_t = pl.BlockSpec((1, T_ALL, LANES), lambda b: (b, 0, 0))
    return pl.pallas_call(
        _route_kernel,
        grid=(bsz,),
        in_specs=[spec],
        out_specs=[spec, spec_t, spec_t],
        out_shape=[jax.ShapeDtypeStruct((bsz, N_EXPERTS, T_ALL), jnp.int32),
                   jax.ShapeDtypeStruct((bsz, T_ALL, LANES), F32),
                   jax.ShapeDtypeStruct((bsz, T_ALL, LANES), F32)],
        compiler_params=_params(("parallel",)),
        name="route",
    )(logits_t)


def _one_hot_rows(srow, n_rows, base):
    r = lax.broadcasted_iota(jnp.int32, (n_rows, srow.shape[1]), 0) + base
    return jnp.where(srow == r, 1.0, 0.0).astype(BF16)


def _gather_kernel(slot_ref, h_ref, xs_ref):
    srow = slot_ref[0, pl.ds(pl.program_id(1), 1), :]
    sel_c = _one_hot_rows(srow[:, 0:CTX_LEN], CAP_CTX, 0)
    xs_ref[0, 0:CAP_CTX, :] = jnp.dot(sel_c, h_ref[0, 0:CTX_LEN, :], preferred_element_type=F32).astype(BF16)
    sel_l = _one_hot_rows(srow[:, CTX_LEN:T_ALL], CAP_LAT, CAP_CTX)
    xs_ref[0, CAP_CTX:CAP_ALL, :] = jnp.dot(sel_l, h_ref[0, CTX_LEN:T_ALL, :],
                                            preferred_element_type=F32).astype(BF16)


def _gather_call(slot, h2):
    bsz = slot.shape[0]
    return pl.pallas_call(
        _gather_kernel,
        grid=(bsz, N_EXPERTS),
        in_specs=[pl.BlockSpec((1, N_EXPERTS, T_ALL), lambda b, e: (b, 0, 0)),
                  pl.BlockSpec((1, T_ALL, D_MODEL), lambda b, e: (b, 0, 0))],
        out_specs=pl.BlockSpec((1, CAP_ALL, D_MODEL), lambda b, e: (e, b, 0)),
        out_shape=jax.ShapeDtypeStruct((N_EXPERTS, bsz * CAP_ALL, D_MODEL), BF16),
        compiler_params=_params(("parallel", "arbitrary")),
        name="moe_gather",
    )(slot, h2)


def _ffn_kernel(x_ref, wu_ref, wg_ref, wd_ref, y_ref, acc_scr, *, row_tile):
    f = pl.program_id(1)
    wu = wu_ref[0, 0].astype(BF16)
    wg = wg_ref[0, 0].astype(BF16)
    wd = wd_ref[0, 0].astype(BF16)
    n_rows = x_ref.shape[1]

    @pl.when(f == 0)
    def _():
        acc_scr[...] = jnp.zeros_like(acc_scr)

    for r0 in range(0, n_rows, row_tile):
        x = x_ref[0, r0:r0 + row_tile, :]
        up = jnp.dot(x, wu, preferred_element_type=F32)
        gt = jnp.dot(x, wg, preferred_element_type=F32)
        acc_scr[r0:r0 + row_tile, :] += jnp.dot((_silu(gt) * up).astype(BF16), wd, preferred_element_type=F32)

    @pl.when(f == EXPERT_FF // FF_TILE - 1)
    def _():
        y_ref[0] = acc_scr[...].astype(BF16)


def _ffn_call(l, xs, w_up, w_gate, w_down):
    n_rows = xs.shape[1]
    row_tile = n_rows // 4
    return pl.pallas_call(
        functools.partial(_ffn_kernel, row_tile=row_tile),
        grid=(N_EXPERTS, EXPERT_FF // FF_TILE),
        in_specs=[
            pl.BlockSpec((1, n_rows, D_MODEL), lambda e, f: (e, 0, 0)),
            pl.BlockSpec((1, 1, D_MODEL, FF_TILE), lambda e, f: (l, e, 0, f)),
            pl.BlockSpec((1, 1, D_MODEL, FF_TILE), lambda e, f: (l, e, 0, f)),
            pl.BlockSpec((1, 1, FF_TILE, D_MODEL), lambda e, f: (l, e, f, 0)),
        ],
        out_specs=pl.BlockSpec((1, n_rows, D_MODEL), lambda e, f: (e, 0, 0)),
        out_shape=jax.ShapeDtypeStruct(xs.shape, BF16),
        scratch_shapes=[pltpu.VMEM((n_rows, D_MODEL), F32)],
        compiler_params=_params(("parallel", "arbitrary")),
        name="moe_ffn",
    )(xs, w_up, w_gate, w_down)


def _scatter_kernel(x_ref, y_ref, slot_ref, gate_ref, mod_ref, o_ref):
    slot_t = slot_ref[0]
    gate_t = gate_ref[0]

    def combine(n_slots, base):
        ids = (lax.broadcasted_iota(jnp.int32, (TOK_TILE, n_slots), 1) + base).astype(F32)
        acc = jnp.zeros((TOK_TILE, D_MODEL), F32)
        for e in range(N_EXPERTS):
            sel = jnp.where(slot_t[:, e:e + 1] == ids, gate_t[:, e:e + 1], 0.0).astype(BF16)
            acc = acc + jnp.dot(sel, y_ref[e, base:base + n_slots, :], preferred_element_type=F32)
        o_ref[0] = x_ref[0] + mod_ref[0, 0, 5:6, :] * acc

    pl.when(pl.program_id(1) == 0)(functools.partial(combine, CAP_CTX, 0))
    pl.when(pl.program_id(1) > 0)(functools.partial(combine, CAP_LAT, CAP_CTX))


def _scatter_call(x, y, slot_t, gate_t, modsel):
    bsz = x.shape[0]
    tok = lambda n: pl.BlockSpec((1, TOK_TILE, n), lambda b, i: (b, i, 0))
    return pl.pallas_call(
        _scatter_kernel,
        grid=(bsz, N_TOK_TILES),
        in_specs=[tok(D_MODEL), pl.BlockSpec((N_EXPERTS, CAP_ALL, D_MODEL), lambda b, i: (0, b, 0)),
                  tok(LANES), tok(LANES),
                  pl.BlockSpec((1, 1, 8, D_MODEL), lambda b, i: (b, jnp.minimum(i, 1), 0, 0))],
        out_specs=tok(D_MODEL),
        out_shape=jax.ShapeDtypeStruct(x.shape, F32),
        compiler_params=_params(("parallel", "arbitrary")),
        name="moe_scatter",
    )(x, y, slot_t, gate_t, modsel)


def _rope_tables(hd, n_tile):
    rows = SEQ // GRID_W
    row = jnp.repeat(jnp.arange(rows, dtype=jnp.int32), GRID_W)
    col = jnp.tile(jnp.arange(GRID_W, dtype=jnp.int32), rows)
    half = hd // 2
    freqs = 1.0 / (ROPE_THETA ** (jnp.arange(0, half, 2, dtype=F32) / half))

    def angles(pos):
        ang = pos.astype(F32)[:, None] * freqs[None, :]
        return jnp.concatenate([ang, ang], axis=-1)

    ang = jnp.concatenate([angles(row), angles(col)], axis=-1)
    cos = jnp.concatenate([jnp.ones((CTX_LEN, hd), F32), jnp.cos(ang)], axis=0)
    sin = jnp.concatenate([jnp.zeros((CTX_LEN, hd), F32), jnp.sin(ang)], axis=0)
    return jnp.tile(cos, (1, n_tile)), jnp.tile(sin, (1, n_tile))


def _group_mean_matrix(group):
    idx = np.arange(BRANCH_W) // group
    return jnp.asarray((idx[:, None] == idx[None, :]).astype(np.float32) / group, BF16)


def _s5_tables(a_re, a_im, log_dt, b_re, b_im, c_re, c_im, bsz):
    dt = jnp.exp(log_dt.astype(F32))[..., None]
    ar, ai = a_re.astype(F32), a_im.astype(F32)
    mag = jnp.exp(ar * dt)
    lr, li = mag * jnp.cos(ai * dt), mag * jnp.sin(ai * dt)
    den = ar * ar + ai * ai
    fr = ((lr - 1.0) * ar + li * ai) / den
    fi = (li * ar - (lr - 1.0) * ai) / den
    bbr = fr[..., None] * b_re - fi[..., None] * b_im
    bbi = fr[..., None] * b_im + fi[..., None] * b_re
    eye = jnp.eye(S5_G, dtype=F32)
    bd_in = lambda m: jnp.einsum('ldgph,gk->ldghkp', m, eye).reshape(DEPTH, 2, BRANCH_W, S5_STATE)
    bmat = jnp.concatenate([bd_in(bbr), bd_in(bbi)], axis=-1).astype(BF16)
    bd_out = lambda m: jnp.einsum('ldghp,gk->ldgpkh', m, eye).reshape(DEPTH, 2, S5_STATE, BRANCH_W)
    cmat = jnp.concatenate([bd_out(c_re.astype(F32)), -bd_out(c_im.astype(F32))], axis=-2).astype(BF16)
    lam = jnp.concatenate([lr.reshape(DEPTH, 2, 1, S5_STATE), li.reshape(DEPTH, 2, 1, S5_STATE)], axis=-1)
    lam = jnp.broadcast_to(lam, (DEPTH, 2, bsz, 2 * S5_STATE))
    return bmat, lam, cmat


def _mix_weights(w_in):
    rep = B_HEADS // 2
    cut = lambda lo, n: w_in[:, :, lo:lo + n]
    dup = lambda lo: jnp.concatenate([cut(lo + (h // rep) * B_HD, B_HD) for h in range(B_HEADS)], axis=-1)
    dt_cols = jnp.pad(cut(2304, 2 * M2_HEADS), ((0, 0), (0, 0), (0, LANES - 2 * M2_HEADS)))
    w_mix = jnp.concatenate([cut(0, 1024), dup(1024), dup(1152), cut(1280, 1024), dt_cols], axis=-1)
    return w_mix.astype(BF16), w_in[:, :, 2312:].astype(BF16)


def _prepare(bsz, c, c_ctx, w_ada, b_ada, norm_g, w_in, da_q_g, da_k_g, da_lambda, da_sub_g, gqa_q_g, gqa_k_g,
             s5_a_re, s5_a_im, s5_log_dt, s5_b_re, s5_b_im, s5_c_re, s5_c_im, s5_d, s5_w_glu, s5_b_glu,
             m2_conv_w, m2_conv_b, m2_a_log, m2_dt_bias, m2_d, m2_norm_g, w_branch, w_out, w_router):
    tb = {}
    cc = jnp.concatenate([c, c_ctx[None], jnp.zeros((16 - bsz - 1, D_MODEL), F32)], axis=0)
    mod_all = _ada_call(cc, w_ada, b_ada).reshape(DEPTH, 16, 6, D_MODEL)
    mod_all = jnp.pad(mod_all, ((0, 0), (0, 0), (0, 2), (0, 0)))
    tb["modsel"] = jnp.stack([jnp.broadcast_to(mod_all[:, bsz:bsz + 1], (DEPTH, bsz, 8, D_MODEL)),
                              mod_all[:, :bsz]], axis=2)
    tb["w_mix"], tb["w_gate_in"] = _mix_weights(w_in)
    tb["cosa"], tb["sina"] = _rope_tables(A_HD, BRANCH_W // A_HD)
    tb["cosb"], tb["sinb"] = _rope_tables(B_HD, BRANCH_W // B_HD)
    tb["gmat_a"], tb["gmat_b"] = _group_mean_matrix(A_HD), _group_mean_matrix(B_HD)
    tile = lambda g: jnp.tile(g.astype(F32), (1, BRANCH_W // g.shape[-1]))
    gains = jnp.stack([tile(da_q_g), tile(da_k_g), tile(gqa_q_g), tile(gqa_k_g)], axis=1)
    tb["gains"] = jnp.pad(gains, ((0, 0), (0, 4), (0, 0)))
    amax = lambda g: jnp.max(jnp.abs(g.astype(F32)), axis=-1)
    tb["logit_bounds"] = 1.05 * jnp.stack([math.sqrt(A_HD) * amax(da_q_g) * amax(da_k_g),
                                           math.sqrt(B_HD) * amax(gqa_q_g) * amax(gqa_k_g)], axis=1).reshape(-1)
    lf = da_lambda.astype(F32)
    tb["lam_init"] = [0.8 - 0.6 * math.exp(-0.3 * l) for l in range(DEPTH)]
    tb["lam"] = (jnp.exp(jnp.sum(lf[:, 0] * lf[:, 1], axis=-1)) - jnp.exp(jnp.sum(lf[:, 2] * lf[:, 3], axis=-1))
                 + jnp.asarray(tb["lam_init"], F32)).reshape(DEPTH, 1, 1)
    row3 = lambda t: t.astype(F32).reshape(DEPTH, 1, t.shape[-1])
    tb["sub_g"] = row3(tile(da_sub_g))
    tb["s5_bmat"], tb["s5_lam"], tb["s5_cmat"] = _s5_tables(
        s5_a_re, s5_a_im, s5_log_dt, s5_b_re, s5_b_im, s5_c_re, s5_c_im, bsz)
    tb["s5_d"], tb["s5_b_glu"], tb["s5_w_glu"] = row3(s5_d), row3(s5_b_glu), s5_w_glu.astype(BF16)
    lane_row = lambda t: jnp.pad(t.astype(F32).reshape(DEPTH, 1, 2 * M2_HEADS),
                                 ((0, 0), (0, 0), (0, LANES - 2 * M2_HEADS)))
    tb["m2_a"] = lane_row(-jnp.exp(m2_a_log.astype(F32)))
    tb["m2_bias"] = lane_row(m2_dt_bias)
    tb["m2_dskip"] = row3(jnp.repeat(m2_d.astype(F32), M2_HD, axis=-1))
    tb["conv_w"] = jnp.pad(m2_conv_w.astype(F32), ((0, 0), (0, 8 - M2_CONV), (0, 0)))
    tb["conv_b"] = row3(m2_conv_b)
    tb["m2_norm_g"] = row3(m2_norm_g)
    tb["norm_rows"] = norm_g.astype(F32).reshape(DEPTH * 2, 1, D_MODEL)
    tb["w_branch"], tb["w_out"] = w_branch.astype(BF16), w_out.astype(BF16)
    tb["w_router_t"] = jnp.swapaxes(w_router.astype(F32), 1, 2)
    return tb


def _mixers(l, xs, tb):
    bsz = xs.shape[0]
    qa, ka, va, qb, kb, vb, u_tb, z, xbc, dt = _inproj_call(
        l, xs, tb["modsel"][l], tb["norm_rows"], tb["w_mix"], tb["gains"], tb["gmat_a"], tb["gmat_b"],
        tb["cosa"], tb["sina"], tb["cosb"], tb["sinb"])
    ya = _attn_call(l, tb["logit_bounds"], qa, ka, va, tb["lam"], tb["sub_g"], tb["gmat_b"], 2,
                    1.0 - tb["lam_init"][l], "diff_attn")
    yb = _attn_call(l, tb["logit_bounds"], qb, kb, vb, tb["lam"], tb["sub_g"], tb["gmat_b"], 1, None, "gqa_attn")
    u_rows = u_tb.reshape(T_ALL * bsz, BRANCH_W)
    y_s5 = _s5_call(u_rows, tb["s5_bmat"][l], tb["s5_lam"][l], tb["s5_cmat"][l], bsz)
    ys = _s5_glu_call(l, u_rows, y_s5, tb["s5_d"], tb["s5_w_glu"], tb["s5_b_glu"], bsz)
    ys = ys.reshape(T_ALL, bsz * BRANCH_W)
    ssd_args = (xbc, dt, tb["conv_w"], tb["conv_b"], tb["m2_a"], tb["m2_bias"], tb["m2_dskip"])
    y_fwd = _ssd_call(l, 0, *ssd_args)
    ym = _ssd_call(l, 1, *ssd_args, z=z, yprev=y_fwd, norm_g=tb["m2_norm_g"])
    return ya, yb, ys, ym


def _layer(l, xs, tb, w_up, w_gate, w_down):
    modsel = tb["modsel"][l]
    ya, yb, ys, ym = _mixers(l, xs, tb)
    x_mid, h2, logits_t = _merge_call(l, xs, modsel, tb["norm_rows"], ya, yb, ys, ym,
                                      tb["w_gate_in"], tb["w_branch"], tb["w_out"], tb["w_router_t"])
    slot, slot_t, gate_t = _route_call(logits_t)
    xe = _gather_call(slot, h2)
    ye = _ffn_call(l, xe, w_up, w_gate, w_down)
    return _scatter_call(x_mid, ye, slot_t, gate_t, modsel)


def kernel(x, c, ctx, c_ctx, w_ada, b_ada, norm_g, w_in, da_q_g, da_k_g, da_lambda, da_sub_g, gqa_q_g, gqa_k_g,
           s5_a_re, s5_a_im, s5_log_dt, s5_b_re, s5_b_im, s5_c_re, s5_c_im, s5_d, s5_w_glu, s5_b_glu,
           m2_conv_w, m2_conv_b, m2_a_log, m2_dt_bias, m2_d, m2_norm_g, w_branch, w_out, w_router,
           w_up, w_gate, w_down):
    bsz = x.shape[0]
    assert x.shape == (bsz, SEQ, D_MODEL) and ctx.shape == (bsz, CTX_LEN, D_MODEL) and bsz == 8
    tb = _prepare(bsz, c, c_ctx, w_ada, b_ada, norm_g, w_in, da_q_g, da_k_g, da_lambda, da_sub_g, gqa_q_g,
                  gqa_k_g, s5_a_re, s5_a_im, s5_log_dt, s5_b_re, s5_b_im, s5_c_re, s5_c_im, s5_d, s5_w_glu,
                  s5_b_glu, m2_conv_w, m2_conv_b, m2_a_log, m2_dt_bias, m2_d, m2_norm_g, w_branch, w_out,
                  w_router)
    xs = jnp.concatenate([ctx, x], axis=1)
    for l in range(DEPTH):
        xs = _layer(l, xs, tb, w_up, w_gate, w_down)
    return xs[:, CTX_LEN:, :]
```

```python
import functools
import math

import jax
import jax.numpy as jnp
import numpy as np
from jax import lax
from jax.experimental import pallas as pl
from jax.experimental.pallas import tpu as pltpu

F32 = jnp.float32
BF16 = jnp.bfloat16
HI = lax.Precision.HIGHEST

D_MODEL = 1024
SEQ = 2048
DEPTH = 4
GRID_W = 64
CTX_LEN = 256
T_ALL = CTX_LEN + SEQ
BRANCH_W = 256
ROPE_THETA = 10000.0
EPS = 1e-6
A_HD = 32
A_VD = 64
A_HEADS = 4
B_HEADS = 4
B_HD = 64
S5_H = 16
S5_G = 16
S5_P = 64
S5_STATE = S5_G * S5_P
M2_HD = 64
M2_HEADS = 4
M2_GROUPS = 2
M2_N = 64
M2_CONV = 5
M2_XBC = 512
N_EXPERTS = 16
EXPERT_FF = 2 * D_MODEL
EC_FACTOR = 2
CAP_CTX = EC_FACTOR * CTX_LEN // N_EXPERTS
CAP_LAT = EC_FACTOR * SEQ // N_EXPERTS
CAP_ALL = CAP_CTX + CAP_LAT

LANES = 128
TOK_TILE = 256
N_TOK_TILES = T_ALL // TOK_TILE
CHUNK = 128
N_CHUNKS = T_ALL // CHUNK
CTX_CHUNKS = CTX_LEN // CHUNK
MIX_COLS = 2944
FF_TILE = 512
VMEM_LIMIT = 56 * 1024 * 1024
LOG2E = 1.0 / math.log(2.0)
SAFE_LOGIT = 40.0


def _params(sem, vmem=VMEM_LIMIT):
    return pltpu.CompilerParams(dimension_semantics=sem, vmem_limit_bytes=vmem)


def _sigmoid(x):
    return 1.0 / (1.0 + jnp.exp(-x))


def _silu(x):
    return x * _sigmoid(x)


def _rms_rows(x):
    return x * lax.rsqrt(jnp.mean(x * x, axis=-1, keepdims=True) + EPS)


def _ada_kernel(c_ref, w_ref, b_ref, o_ref):
    sc = _silu(c_ref[...])
    o_ref[0] = jnp.dot(sc, w_ref[0], precision=HI, preferred_element_type=F32) + b_ref[0]


def _ada_call(cc, w_ada, b_ada):
    tn = 1536
    return pl.pallas_call(
        _ada_kernel,
        grid=(DEPTH, 6 * D_MODEL // tn),
        in_specs=[
            pl.BlockSpec((16, D_MODEL), lambda l, j: (0, 0)),
            pl.BlockSpec((1, D_MODEL, tn), lambda l, j: (l, 0, j)),
            pl.BlockSpec((1, 1, tn), lambda l, j: (l, 0, j)),
        ],
        out_specs=pl.BlockSpec((1, 16, tn), lambda l, j: (l, 0, j)),
        out_shape=jax.ShapeDtypeStruct((DEPTH, 16, 6 * D_MODEL), F32),
        compiler_params=_params(("arbitrary", "arbitrary")),
        name="ada_mod",
    )(cc, w_ada, b_ada.reshape(DEPTH, 1, 6 * D_MODEL))


def _group_mean(x, gmat):
    hi = x.astype(BF16)
    lo = (x - hi.astype(F32)).astype(BF16)
    return jnp.dot(hi, gmat, preferred_element_type=F32) + jnp.dot(lo, gmat, preferred_element_type=F32)


def _group_norm_rope(x, gmat, gain, cos, sin, shift):
    xn = x * lax.rsqrt(_group_mean(x * x, gmat) + EPS) * gain
    lane = lax.broadcasted_iota(jnp.int32, xn.shape, 1)
    up = pltpu.roll(xn, BRANCH_W - shift, 1)
    dn = pltpu.roll(xn, shift, 1)
    rot = jnp.where((lane % (2 * shift)) < shift, -up, dn)
    return xn * cos + rot * sin


def _inproj_kernel(x_ref, mod_ref, g_ref, w_ref, gains_ref, ga_ref, gb_ref,
                   cosa_ref, sina_ref, cosb_ref, sinb_ref,
                   qa_ref, ka_ref, va_ref, qb_ref, kb_ref, vb_ref, u_ref, z_ref, xbc_ref, dt_ref):
    mod = mod_ref[0, 0]
    h = _rms_rows(x_ref[0]) * g_ref[0] * (1.0 + mod[1:2]) + mod[0:1]
    p = jnp.dot(h.astype(BF16), w_ref[0], preferred_element_type=F32)
    gains = gains_ref[0]
    w = BRANCH_W
    qa = _group_norm_rope(p[:, 0:w], ga_ref[...], gains[0:1], cosa_ref[...], sina_ref[...], A_HD // 4)
    ka = _group_norm_rope(p[:, w:2 * w], ga_ref[...], gains[1:2], cosa_ref[...], sina_ref[...], A_HD // 4)
    qb = _group_norm_rope(p[:, 3 * w:4 * w], gb_ref[...], gains[2:3], cosb_ref[...], sinb_ref[...], B_HD // 4)
    kb = _group_norm_rope(p[:, 4 * w:5 * w], gb_ref[...], gains[3:4], cosb_ref[...], sinb_ref[...], B_HD // 4)
    qa_ref[0] = (qa * (LOG2E / math.sqrt(A_HD))).astype(BF16)
    ka_ref[0] = ka.astype(BF16)
    va_ref[0] = p[:, 2 * w:3 * w].astype(BF16)
    qb_ref[0] = (qb * (LOG2E / math.sqrt(B_HD))).astype(BF16)
    kb_ref[0] = kb.astype(BF16)
    vb_ref[0] = p[:, 5 * w:6 * w].astype(BF16)
    u_ref[...] = p[:, 6 * w:7 * w].astype(BF16)
    z_ref[0] = p[:, 7 * w:8 * w]
    xbc_ref[0] = p[:, 8 * w:10 * w]
    dt_ref[0] = p[:, 10 * w:10 * w + LANES]


def _inproj_call(l, x, modsel, norm_g, w_mix, gains, gmat_a, gmat_b, cosa, sina, cosb, sinb):
    bsz = x.shape[0]
    w = BRANCH_W
    tok = lambda n: pl.BlockSpec((1, TOK_TILE, n), lambda b, i: (b, i, 0))
    tab = pl.BlockSpec((TOK_TILE, w), lambda b, i: (i, 0))
    full2 = lambda a: pl.BlockSpec(a.shape, lambda b, i: (0, 0))
    out_bt = lambda n, dt: jax.ShapeDtypeStruct((bsz, T_ALL, n), dt)
    return pl.pallas_call(
        _inproj_kernel,
        grid=(bsz, N_TOK_TILES),
        in_specs=[
            tok(D_MODEL),
            pl.BlockSpec((1, 1, 8, D_MODEL), lambda b, i: (b, jnp.minimum(i, 1), 0, 0)),
            pl.BlockSpec((1, 1, D_MODEL), lambda b, i: (2 * l, 0, 0)),
            pl.BlockSpec((1, D_MODEL, MIX_COLS), lambda b, i: (l, 0, 0)),
            pl.BlockSpec((1, 8, w), lambda b, i: (l, 0, 0)),
            full2(gmat_a), full2(gmat_b), tab, tab, tab, tab,
        ],
        out_specs=[tok(w), tok(w), tok(w), tok(w), tok(w), tok(w),
                   pl.BlockSpec((TOK_TILE, w), lambda b, i: (i, b)),
                   tok(w), tok(M2_XBC), tok(LANES)],
        out_shape=[out_bt(w, BF16)] * 6 + [jax.ShapeDtypeStruct((T_ALL, bsz * w), BF16),
                                            out_bt(w, F32), out_bt(M2_XBC, F32), out_bt(LANES, F32)],
        compiler_params=_params(("parallel", "parallel")),
        name="in_proj",
    )(x, modsel, norm_g, w_mix, gains, gmat_a, gmat_b, cosa, sina, cosb, sinb)


def _attn_kernel(bound_ref, q_ref, k_ref, v_ref, lam_ref, subg_ref, gmat_ref, o_ref, *,
                 n_maps, post_scale, bound_idx):
    lane = lax.broadcasted_iota(jnp.int32, (TOK_TILE, BRANCH_W), 1)
    q = q_ref[0]
    map_w = A_VD // n_maps

    def attend(n_keys, shift):
        k = k_ref[0, 0:n_keys, :]
        v = v_ref[0, 0:n_keys, :]
        out = jnp.zeros((TOK_TILE, BRANCH_W), F32)
        for head in range(A_HEADS):
            acc = None
            for m in range(n_maps):
                lo = head * A_VD + m * map_w
                qm = jnp.where((lane >= lo) & (lane < lo + map_w), q, jnp.zeros_like(q))
                s = lax.dot_general(qm, k, (((1,), (1,)), ((), ())), preferred_element_type=F32)
                if shift:
                    s = s - jnp.max(s, axis=-1, keepdims=True)
                e = jnp.exp2(s)
                r = 1.0 / jnp.sum(e, axis=-1, keepdims=True)
                o = jnp.dot(e.astype(BF16), v, preferred_element_type=F32)
                acc = o * r if m == 0 else acc - o * (r * lam_ref[0])
            out = jnp.where((lane >= head * A_VD) & (lane < (head + 1) * A_VD), acc, out)
        if post_scale is not None:
            out = out * lax.rsqrt(_group_mean(out * out, gmat_ref[...]) + EPS) * subg_ref[0] * post_scale
        o_ref[0] = out.astype(BF16)

    is_ctx = pl.program_id(1) == 0
    small = bound_ref[bound_idx] <= SAFE_LOGIT
    for n_keys, tile_sel in ((CTX_LEN, is_ctx), (T_ALL, jnp.logical_not(is_ctx))):
        for shift, bound_sel in ((False, small), (True, jnp.logical_not(small))):
            pl.when(jnp.logical_and(tile_sel, bound_sel))(functools.partial(attend, n_keys, shift))


def _attn_call(l, bounds, q, k, v, lam, subg, gmat, n_maps, post_scale, name):
    bsz = q.shape[0]
    w = BRANCH_W
    return pl.pallas_call(
        functools.partial(_attn_kernel, n_maps=n_maps, post_scale=post_scale,
                          bound_idx=2 * l + (0 if n_maps == 2 else 1)),
        grid=(bsz, N_TOK_TILES),
        in_specs=[
            pl.BlockSpec(memory_space=pltpu.SMEM),
            pl.BlockSpec((1, TOK_TILE, w), lambda b, i: (b, i, 0)),
            pl.BlockSpec((1, T_ALL, w), lambda b, i: (b, 0, 0)),
            pl.BlockSpec((1, T_ALL, w), lambda b, i: (b, 0, 0)),
            pl.BlockSpec((1, 1, 1), lambda b, i: (l, 0, 0)),
            pl.BlockSpec((1, 1, w), lambda b, i: (l, 0, 0)),
            pl.BlockSpec(gmat.shape, lambda b, i: (0, 0)),
        ],
        out_specs=pl.BlockSpec((1, TOK_TILE, w), lambda b, i: (b, i, 0)),
        out_shape=jax.ShapeDtypeStruct((bsz, T_ALL, w), BF16),
        compiler_params=_params(("parallel", "parallel")),
        name=name,
    )(bounds, q, k, v, lam, subg, gmat)


def _scan_chunk_index(direction, j):
    back = jnp.where(j < CTX_CHUNKS, CTX_CHUNKS - 1 - j, N_CHUNKS + CTX_CHUNKS - 1 - j)
    return jnp.where(direction == 0, j, back)


def _s5_kernel(u_ref, bmat_ref, lam_ref, cmat_ref, y_ref, bu_scr, h_scr, *, bsz):
    direction = pl.program_id(0)

    @pl.when(pl.program_id(1) == 0)
    def _():
        h_scr[...] = jnp.zeros_like(h_scr)

    bu_scr[...] = jnp.dot(u_ref[...], bmat_ref[0], preferred_element_type=F32)
    lam_re = lam_ref[0, :, 0:S5_STATE]
    lam_im = lam_ref[0, :, S5_STATE:2 * S5_STATE]

    def step(t, carry):
        h_re, h_im = carry
        tt = jnp.where(direction == 0, t, CHUNK - 1 - t)
        rows = pl.ds(pl.multiple_of(tt * bsz, bsz), bsz)
        n_re = lam_re * h_re - lam_im * h_im + bu_scr[rows, 0:S5_STATE]
        n_im = lam_re * h_im + lam_im * h_re + bu_scr[rows, S5_STATE:2 * S5_STATE]
        bu_scr[rows, 0:S5_STATE] = n_re
        bu_scr[rows, S5_STATE:2 * S5_STATE] = n_im
        return n_re, n_im

    h_re, h_im = lax.fori_loop(0, CHUNK, step, (h_scr[:, 0:S5_STATE], h_scr[:, S5_STATE:2 * S5_STATE]),
                               unroll=4)
    h_scr[:, 0:S5_STATE] = h_re
    h_scr[:, S5_STATE:2 * S5_STATE] = h_im
    y_ref[0] = jnp.dot(bu_scr[...].astype(BF16), cmat_ref[0], preferred_element_type=F32)


def _s5_call(u_tb, bmat, lam, cmat, bsz):
    rows = CHUNK * bsz
    return pl.pallas_call(
        functools.partial(_s5_kernel, bsz=bsz),
        grid=(2, N_CHUNKS),
        in_specs=[
            pl.BlockSpec((rows, BRANCH_W), lambda d, j: (_scan_chunk_index(d, j), 0)),
            pl.BlockSpec((1, BRANCH_W, 2 * S5_STATE), lambda d, j: (d, 0, 0)),
            pl.BlockSpec((1, bsz, 2 * S5_STATE), lambda d, j: (d, 0, 0)),
            pl.BlockSpec((1, 2 * S5_STATE, BRANCH_W), lambda d, j: (d, 0, 0)),
        ],
        out_specs=pl.BlockSpec((1, rows, BRANCH_W), lambda d, j: (d, _scan_chunk_index(d, j), 0)),
        out_shape=jax.ShapeDtypeStruct((2, T_ALL * bsz, BRANCH_W), F32),
        scratch_shapes=[pltpu.VMEM((rows, 2 * S5_STATE), F32), pltpu.VMEM((bsz, 2 * S5_STATE), F32)],
        compiler_params=_params(("arbitrary", "arbitrary")),
        name="s5_scan",
    )(u_tb, bmat, lam, cmat)


def _s5_glu_kernel(u_ref, y_ref, d_ref, w_ref, b_ref, o_ref):
    t = d_ref[0] * u_ref[...].astype(F32) + y_ref[0] + y_ref[1]
    t = 0.5 * t * (1.0 + jnp.tanh(math.sqrt(2.0 / math.pi) * (t + 0.044715 * (t * t * t))))
    gate = jnp.dot(t.astype(BF16), w_ref[0], preferred_element_type=F32) + b_ref[0]
    o_ref[...] = (t * _sigmoid(gate)).astype(BF16)


def _s5_glu_call(l, u_tb, y, s5_d, w_glu, b_glu, bsz):
    rows = CHUNK * bsz
    w = BRANCH_W
    return pl.pallas_call(
        _s5_glu_kernel,
        grid=(N_CHUNKS,),
        in_specs=[
            pl.BlockSpec((rows, w), lambda j: (j, 0)),
            pl.BlockSpec((2, rows, w), lambda j: (0, j, 0)),
            pl.BlockSpec((1, 1, w), lambda j: (l, 0, 0)),
            pl.BlockSpec((1, w, w), lambda j: (l, 0, 0)),
            pl.BlockSpec((1, 1, w), lambda j: (l, 0, 0)),
        ],
        out_specs=pl.BlockSpec((rows, w), lambda j: (j, 0)),
        out_shape=jax.ShapeDtypeStruct((T_ALL * bsz, w), BF16),
        compiler_params=_params(("parallel",)),
        name="s5_glu",
    )(u_tb, y, s5_d, w_glu, b_glu)


PAD_ROWS = 8


def _softplus(x):
    return jnp.maximum(x, 0.0) + jnp.log(1.0 + jnp.exp(-jnp.abs(x)))


def _ssd_kernel(*refs, direction, final):
    if final:
        (xbc_ref, dt_ref, cw_ref, cb_ref, a_ref, bias_ref, dskip_ref, z_ref, yprev_ref, ng_ref,
         o_ref, pad_scr, xact_scr, dtv_scr, ad_scr, st_scr) = refs
    else:
        (xbc_ref, dt_ref, cw_ref, cb_ref, a_ref, bias_ref, dskip_ref,
         o_ref, pad_scr, xact_scr, dtv_scr, ad_scr, st_scr) = refs
    rev = direction == 1

    lat0 = CTX_LEN + 3 * PAD_ROWS - PAD_ROWS
    pad_scr[...] = jnp.zeros_like(pad_scr)
    pad_scr[PAD_ROWS:PAD_ROWS + CTX_LEN, :] = xbc_ref[0, 0:CTX_LEN, :]
    pad_scr[lat0:lat0 + SEQ, :] = xbc_ref[0, CTX_LEN:T_ALL, :]
    for c in range(N_CHUNKS):
        base = c * CHUNK + (PAD_ROWS if c < CTX_CHUNKS else lat0 - CTX_LEN)
        acc = jnp.broadcast_to(cb_ref[0], (CHUNK, M2_XBC))
        for tap in range(M2_CONV):
            start = base + tap - M2_CONV // 2
            acc = acc + cw_ref[0, tap:tap + 1, :] * pad_scr[start:start + CHUNK, :]
        xact_scr[c * CHUNK:(c + 1) * CHUNK, :] = _silu(acc)

    dtv = _softplus(dt_ref[0] + bias_ref[0])
    dtv_scr[...] = dtv
    ad_scr[...] = dtv * a_ref[0]
    st_scr[...] = jnp.zeros_like(st_scr)

    ti = lax.broadcasted_iota(jnp.int32, (CHUNK, CHUNK), 0)
    si = lax.broadcasted_iota(jnp.int32, (CHUNK, CHUNK), 1)
    tri = jnp.where(ti >= si, 1.0, 0.0).astype(F32)
    keep = (si >= ti) if rev else (ti >= si)

    def chunk_body(j, carry):
        ci = _scan_chunk_index(direction, j)
        rows = pl.ds(pl.multiple_of(ci * CHUNK, CHUNK), CHUNK)
        xa = xact_scr[rows, :]
        dtc = dtv_scr[rows, :]
        adc = ad_scr[rows, :]
        cs = jnp.dot(tri, adc, precision=HI, preferred_element_type=F32)
        tot = cs[CHUNK - 1:CHUNK, :]
        pcs = cs - adc if rev else cs
        pcs_t = jnp.transpose(pcs)
        ys = []
        for grp in range(M2_GROUPS):
            b0 = BRANCH_W + grp * M2_N
            c0 = BRANCH_W + M2_GROUPS * M2_N + grp * M2_N
            bg = xa[:, b0:b0 + M2_N]
            cg = xa[:, c0:c0 + M2_N].astype(BF16)
            gmat = lax.dot_general(cg, bg.astype(BF16), (((1,), (1,)), ((), ())), preferred_element_type=F32)
            for hh in range(M2_HEADS // M2_GROUPS):
                head = grp * (M2_HEADS // M2_GROUPS) + hh
                ch = direction * M2_HEADS + head
                col = pcs[:, ch:ch + 1]
                row = pcs_t[ch:ch + 1, :]
                tot_h = tot[:, ch:ch + 1]
                arg = (row - col) if rev else (col - row)
                lmat = jnp.exp(jnp.where(keep, arg, -jnp.inf))
                xd = (xa[:, head * M2_HD:(head + 1) * M2_HD] * dtc[:, ch:ch + 1]).astype(BF16)
                y_diag = jnp.dot((gmat * lmat).astype(BF16), xd, preferred_element_type=F32)
                off = jnp.exp(tot_h - col) if rev else jnp.exp(col)
                dte = jnp.exp(col) if rev else jnp.exp(tot_h - col)
                state = st_scr[head]
                y_off = jnp.dot(cg, state.astype(BF16), preferred_element_type=F32) * off
                st_scr[head] = jnp.exp(tot_h) * state + lax.dot_general(
                    (bg * dte).astype(BF16), xd, (((0,), (0,)), ((), ())), preferred_element_type=F32)
                ys.append(y_diag + y_off)
        y = jnp.concatenate(ys, axis=-1)
        if final:
            y = y + yprev_ref[0, rows, :]
            g = y * _silu(z_ref[0, rows, :])
            o_ref[0, rows, :] = (_rms_rows(g) * ng_ref[0]).astype(BF16)
        else:
            o_ref[0, rows, :] = y + dskip_ref[0] * xa[:, 0:BRANCH_W]
        return carry

    lax.fori_loop(0, N_CHUNKS, chunk_body, 0)


def _ssd_call(l, direction, xbc, dt, conv_w, conv_b, a_rows, bias_rows, dskip, z=None, yprev=None, norm_g=None):
    bsz = xbc.shape[0]
    final = z is not None
    w = BRANCH_W
    seq = lambda n: pl.BlockSpec((1, T_ALL, n), lambda b: (b, 0, 0))
    lay = lambda r, n: pl.BlockSpec((1, r, n), lambda b: (l, 0, 0))
    in_specs = [seq(M2_XBC), seq(LANES), lay(8, M2_XBC), lay(1, M2_XBC), lay(1, LANES), lay(1, LANES), lay(1, w)]
    args = [xbc, dt, conv_w, conv_b, a_rows, bias_rows, dskip]
    if final:
        in_specs += [seq(w), seq(w), lay(1, w)]
        args += [z, yprev, norm_g]
    return pl.pallas_call(
        functools.partial(_ssd_kernel, direction=direction, final=final),
        grid=(bsz,),
        in_specs=in_specs,
        out_specs=seq(w),
        out_shape=jax.ShapeDtypeStruct((bsz, T_ALL, w), BF16 if final else F32),
        scratch_shapes=[
            pltpu.VMEM((T_ALL + 3 * PAD_ROWS, M2_XBC), F32),
            pltpu.VMEM((T_ALL, M2_XBC), F32),
            pltpu.VMEM((T_ALL, LANES), F32),
            pltpu.VMEM((T_ALL, LANES), F32),
            pltpu.VMEM((M2_HEADS, M2_N, M2_HD), F32),
        ],
        compiler_params=_params(("parallel",)),
        name="ssd_bwd" if final else "ssd_fwd",
    )(*args)


MERGE_TILE = 768


def _merge_kernel(x_ref, mod_ref, g_ref, ya_ref, yb_ref, ys_ref, ym_ref, wg_ref, wb_ref, wo_ref, wr_ref,
                  xo_ref, h2_ref, lg_ref):
    row = pl.program_id(1) * MERGE_TILE + lax.broadcasted_iota(jnp.int32, (MERGE_TILE, 1), 0)
    is_ctx = row < CTX_LEN
    mod = lambda k: jnp.where(is_ctx, mod_ref[0, 0, k:k + 1, :], mod_ref[0, 1, k:k + 1, :])
    x = x_ref[0]
    h = (_rms_rows(x) * g_ref[0] * (1.0 + mod(1)) + mod(0)).astype(BF16)
    acc = jnp.zeros((MERGE_TILE, D_MODEL), F32)
    for n, y_ref in enumerate((ya_ref, yb_ref, ys_ref, ym_ref)):
        yv = y_ref[0] if n != 2 else y_ref[...]
        gate = _sigmoid(jnp.dot(h, wg_ref[0, :, n * D_MODEL:(n + 1) * D_MODEL], preferred_element_type=F32))
        acc = acc + gate * jnp.dot(yv, wb_ref[0, n], preferred_element_type=F32)
    xn = x + mod(2) * jnp.dot(acc.astype(BF16), wo_ref[0], preferred_element_type=F32)
    xo_ref[0] = xn
    h2 = _rms_rows(xn) * g_ref[1] * (1.0 + mod(4)) + mod(3)
    h2_ref[0] = h2.astype(BF16)
    lg_ref[0] = lax.dot_general(wr_ref[0], h2, (((1,), (1,)), ((), ())), precision=HI, preferred_element_type=F32)


def _merge_call(l, x, modsel, norm_g, ya, yb, ys, ym, w_gate_in, w_branch, w_out, w_router_t):
    bsz = x.shape[0]
    w = BRANCH_W
    tok = lambda n: pl.BlockSpec((1, MERGE_TILE, n), lambda b, i: (b, i, 0))
    return pl.pallas_call(
        _merge_kernel,
        grid=(bsz, T_ALL // MERGE_TILE),
        in_specs=[
            tok(D_MODEL),
            pl.BlockSpec((1, 2, 8, D_MODEL), lambda b, i: (b, 0, 0, 0)),
            pl.BlockSpec((2, 1, D_MODEL), lambda b, i: (l, 0, 0)),
            tok(w), tok(w), pl.BlockSpec((MERGE_TILE, w), lambda b, i: (i, b)), tok(w),
            pl.BlockSpec((1, D_MODEL, 4 * D_MODEL), lambda b, i: (l, 0, 0)),
            pl.BlockSpec((1, 4, w, D_MODEL), lambda b, i: (l, 0, 0, 0)),
            pl.BlockSpec((1, D_MODEL, D_MODEL), lambda b, i: (l, 0, 0)),
            pl.BlockSpec((1, N_EXPERTS, D_MODEL), lambda b, i: (l, 0, 0)),
        ],
        out_specs=[tok(D_MODEL), tok(D_MODEL), pl.BlockSpec((1, N_EXPERTS, MERGE_TILE), lambda b, i: (b, 0, i))],
        out_shape=[jax.ShapeDtypeStruct((bsz, T_ALL, D_MODEL), F32),
                   jax.ShapeDtypeStruct((bsz, T_ALL, D_MODEL), BF16),
                   jax.ShapeDtypeStruct((bsz, N_EXPERTS, T_ALL), F32)],
        compiler_params=_params(("parallel", "parallel")),
        name="merge",
    )(x, modsel, norm_g, ya, yb, ys, ym, w_gate_in, w_branch, w_out, w_router_t)


BISECT_STEPS = 48
SLOT_WIN = 64
SLOT_ALIGN = 16
MAX_PASSES = -(-(CAP_LAT + SLOT_ALIGN - 1) // SLOT_WIN)


def _route_kernel(lg_ref, slot_ref, slot_t_ref, gate_t_ref, win_ref, npass_ref):
    lg = lg_ref[0]
    sh = lg - jnp.max(lg, axis=0, keepdims=True)
    ex = jnp.exp(sh)
    den = jnp.sum(ex, axis=0, keepdims=True)
    aff = ex / den
    logaff = sh - jnp.log(den)
    ri = lax.broadcasted_iota(jnp.int32, (LANES, LANES), 0)
    ci = lax.broadcasted_iota(jnp.int32, (LANES, LANES), 1)
    upper = jnp.where(ri <= ci, 1.0, 0.0).astype(BF16)

    def prefix_exclusive(mask, lo, hi):
        carry = jnp.zeros((N_EXPERTS, 1), F32)
        parts = []
        for blk in range((hi - lo) // LANES):
            m = mask[:, blk * LANES:(blk + 1) * LANES]
            inc = jnp.dot(m.astype(BF16), upper, preferred_element_type=F32)
            parts.append(inc - m + carry)
            carry = carry + inc[:, LANES - 1:LANES]
        return jnp.concatenate(parts, axis=1)

    segments = ((0, CTX_LEN, CAP_CTX, 0), (CTX_LEN, T_ALL, CAP_LAT, CAP_CTX))

    def bisect(_, bounds):
        out = []
        for (lo, hi, cap, _), (low, high) in zip(segments, bounds):
            mid = 0.5 * (low + high)
            cnt = jnp.sum(jnp.where(logaff[:, lo:hi] >= mid, 1.0, 0.0), axis=1, keepdims=True)
            ok = cnt >= cap
            out.append((jnp.where(ok, mid, low), jnp.where(ok, high, mid)))
        return tuple(out)

    start = tuple((jnp.min(logaff[:, lo:hi], axis=1, keepdims=True), jnp.ones((N_EXPERTS, 1), F32))
                  for lo, hi, _, _ in segments)
    bounds = lax.fori_loop(0, BISECT_STEPS, bisect, start)

    def choose(lo, hi, cap, base, low, high):
        seg = logaff[:, lo:hi]
        gt = jnp.where(seg >= high, 1.0, 0.0)
        eq = jnp.where(seg >= low, 1.0, 0.0) - gt
        need = cap - jnp.sum(gt, axis=1, keepdims=True)
        sel = gt + eq * jnp.where(prefix_exclusive(eq, lo, hi) < need, 1.0, 0.0)
        pos = prefix_exclusive(sel, lo, hi) + base
        slot = jnp.where(sel > 0.0, pos, -1.0)
        gate = sel * aff[:, lo:hi]
        slot_ref[0, :, lo:hi] = slot.astype(jnp.int32)
        fill = jnp.zeros((LANES - N_EXPERTS, LANES), F32)
        for blk in range((hi - lo) // LANES):
            cols = slice(blk * LANES, (blk + 1) * LANES)
            rows = slice(lo + blk * LANES, lo + (blk + 1) * LANES)
            slot_t_ref[0, rows, :] = jnp.transpose(jnp.concatenate([slot[:, cols], fill], axis=0))
            gate_t_ref[0, rows, :] = jnp.transpose(jnp.concatenate([gate[:, cols], fill], axis=0))
        tiles = []
        for t in range((hi - lo) // TOK_TILE):
            cols = slice(t * TOK_TILE, (t + 1) * TOK_TILE)
            chosen = sel[:, cols] > 0.0
            first = jnp.min(jnp.where(chosen, pos[:, cols], float(CAP_ALL)), axis=1, keepdims=True)
            last = jnp.max(jnp.where(chosen, pos[:, cols], -1.0), axis=1, keepdims=True)
            start = jnp.minimum(jnp.floor(first * (1.0 / SLOT_ALIGN)) * SLOT_ALIGN, float(CAP_ALL - SLOT_WIN))
            tiles.append((lo // TOK_TILE + t, start, jnp.floor((last - start + SLOT_WIN) * (1.0 / SLOT_WIN))))
        return tiles

    lane = lax.broadcasted_iota(jnp.int32, (N_EXPERTS, LANES), 1)
    win = jnp.zeros((N_EXPERTS, LANES), F32)
    npass = jnp.zeros((N_EXPERTS, LANES), F32)
    for seg_def, (low, high) in zip(segments, bounds):
        for j, start, passes in choose(*seg_def, low, high):
            win = jnp.where(lane == j, start, win)
            npass = jnp.where(lane == j, passes, npass)
    win_ref[0] = win.astype(jnp.int32)
    npass_ref[0] = jnp.broadcast_to(jnp.max(npass, axis=0, keepdims=True), (8, LANES)).astype(jnp.int32)


def _route_call(logits_t):
    bsz = logits_t.shape[0]
    spec = pl.BlockSpec((1, N_EXPERTS, T_ALL), lambda b: (b, 0, 0))
    spec_t = pl.BlockSpec((1, T_ALL, LANES), lambda b: (b, 0, 0))
    return pl.pallas_call(
        _route_kernel,
        grid=(bsz,),
        in_specs=[spec],
        out_specs=[spec, spec_t, spec_t, pl.BlockSpec((1, N_EXPERTS, LANES), lambda b: (b, 0, 0)),
                   pl.BlockSpec((1, 8, LANES), lambda b: (b, 0, 0))],
        out_shape=[jax.ShapeDtypeStruct((bsz, N_EXPERTS, T_ALL), jnp.int32),
                   jax.ShapeDtypeStruct((bsz, T_ALL, LANES), F32),
                   jax.ShapeDtypeStruct((bsz, T_ALL, LANES), F32),
                   jax.ShapeDtypeStruct((bsz, N_EXPERTS, LANES), jnp.int32),
                   jax.ShapeDtypeStruct((bsz, 8, LANES), jnp.int32)],
        compiler_params=_params(("parallel",)),
        name="route",
    )(logits_t)


def _window(win_ref, b, j, e, c):
    low = win_ref[(b * N_EXPERTS + e) * N_TOK_TILES + j] + c * SLOT_WIN
    return pl.multiple_of(jnp.minimum(low, CAP_ALL - SLOT_WIN), SLOT_ALIGN), low


def _gather_kernel(win_ref, npass_ref, slot_ref, h_ref, xs_ref):
    b, j = pl.program_id(0), pl.program_id(1)

    @pl.when(j == 0)
    def _():
        xs_ref[...] = jnp.zeros_like(xs_ref)

    def one_pass(c):
        rows = lax.broadcasted_iota(jnp.int32, (SLOT_WIN, TOK_TILE), 0)
        starts, blocks = [], []
        for e in range(N_EXPERTS):
            start, low = _window(win_ref, b, j, e, c)
            ids = rows + start
            hit = jnp.where(slot_ref[0, e:e + 1, :] == ids, jnp.where(ids >= low, 1.0, 0.0), 0.0)
            starts.append(start)
            blocks.append(hit.astype(BF16))
        picked = jnp.dot(jnp.concatenate(blocks, axis=0), h_ref[0], preferred_element_type=F32)
        for e, start in enumerate(starts):
            xs_ref[e, pl.ds(start, SLOT_WIN), :] += picked[e * SLOT_WIN:(e + 1) * SLOT_WIN].astype(BF16)

    one_pass(0)
    for c in range(1, MAX_PASSES):
        pl.when(npass_ref[b * N_TOK_TILES + j] > c)(functools.partial(one_pass, c))


def _gather_call(win, npass, slot, h2):
    bsz = slot.shape[0]
    smem = pl.BlockSpec(memory_space=pltpu.SMEM)
    return pl.pallas_call(
        _gather_kernel,
        grid=(bsz, N_TOK_TILES),
        in_specs=[smem, smem,
                  pl.BlockSpec((1, N_EXPERTS, TOK_TILE), lambda b, j: (b, 0, j)),
                  pl.BlockSpec((1, TOK_TILE, D_MODEL), lambda b, j: (b, j, 0))],
        out_specs=pl.BlockSpec((N_EXPERTS, CAP_ALL, D_MODEL), lambda b, j: (0, b, 0)),
        out_shape=jax.ShapeDtypeStruct((N_EXPERTS, bsz * CAP_ALL, D_MODEL), BF16),
        compiler_params=_params(("parallel", "arbitrary")),
        name="moe_gather",
    )(win, npass, slot, h2)


def _ffn_kernel(x_ref, wu_ref, wg_ref, wd_ref, y_ref, acc_scr, *, row_tile):
    f = pl.program_id(1)
    wu = wu_ref[0, 0].astype(BF16)
    wg = wg_ref[0, 0].astype(BF16)
    wd = wd_ref[0, 0].astype(BF16)
    n_rows = x_ref.shape[1]

    @pl.when(f == 0)
    def _():
        acc_scr[...] = jnp.zeros_like(acc_scr)

    for r0 in range(0, n_rows, row_tile):
        x = x_ref[0, r0:r0 + row_tile, :]
        up = jnp.dot(x, wu, preferred_element_type=F32)
        gt = jnp.dot(x, wg, preferred_element_type=F32)
        acc_scr[r0:r0 + row_tile, :] += jnp.dot((_silu(gt) * up).astype(BF16), wd, preferred_element_type=F32)

    @pl.when(f == EXPERT_FF // FF_TILE - 1)
    def _():
        y_ref[0] = acc_scr[...].astype(BF16)


def _ffn_call(l, xs, w_up, w_gate, w_down):
    n_rows = xs.shape[1]
    row_tile = n_rows // 4
    return pl.pallas_call(
        functools.partial(_ffn_kernel, row_tile=row_tile),
        grid=(N_EXPERTS, EXPERT_FF // FF_TILE),
        in_specs=[
            pl.BlockSpec((1, n_rows, D_MODEL), lambda e, f: (e, 0, 0)),
            pl.BlockSpec((1, 1, D_MODEL, FF_TILE), lambda e, f: (l, e, 0, f)),
            pl.BlockSpec((1, 1, D_MODEL, FF_TILE), lambda e, f: (l, e, 0, f)),
            pl.BlockSpec((1, 1, FF_TILE, D_MODEL), lambda e, f: (l, e, f, 0)),
        ],
        out_specs=pl.BlockSpec((1, n_rows, D_MODEL), lambda e, f: (e, 0, 0)),
        out_shape=jax.ShapeDtypeStruct(xs.shape, BF16),
        scratch_shapes=[pltpu.VMEM((n_rows, D_MODEL), F32)],
        compiler_params=_params(("parallel", "arbitrary")),
        name="moe_ffn",
    )(xs, w_up, w_gate, w_down)


def _scatter_kernel(win_ref, npass_ref, x_ref, y_ref, slot_ref, gate_ref, mod_ref, o_ref, acc_scr):
    b, j = pl.program_id(0), pl.program_id(1)
    slot_t = slot_ref[0]
    gate_t = gate_ref[0].astype(BF16)
    width = N_EXPERTS * SLOT_WIN
    spread = jnp.where(lax.broadcasted_iota(jnp.int32, (LANES, width), 1) // SLOT_WIN
                       == lax.broadcasted_iota(jnp.int32, (LANES, width), 0), 1.0, 0.0).astype(BF16)
    gate_w = jnp.dot(gate_t, spread, preferred_element_type=F32)
    offset = (lax.broadcasted_iota(jnp.int32, (TOK_TILE, width), 1) % SLOT_WIN).astype(F32)
    lane = lax.broadcasted_iota(jnp.int32, (1, LANES), 1)

    def one_pass(c):
        start_row = jnp.zeros((1, LANES), F32)
        low_row = jnp.zeros((1, LANES), F32)
        rows = []
        for e in range(N_EXPERTS):
            start, low = _window(win_ref, b, j, e, c)
            start_row = jnp.where(lane == e, start.astype(F32), start_row)
            low_row = jnp.where(lane == e, low.astype(F32), low_row)
            rows.append(y_ref[e, pl.ds(start, SLOT_WIN), :])
        rel = jnp.where(slot_t >= low_row, jnp.minimum(slot_t - start_row, float(SLOT_WIN)), -1.0)
        rel_w = jnp.dot(rel.astype(BF16), spread, preferred_element_type=F32)
        sel = jnp.where(rel_w == offset, gate_w, 0.0).astype(BF16)
        part = jnp.dot(sel, jnp.concatenate(rows, axis=0), preferred_element_type=F32)
        if c == 0:
            acc_scr[...] = part
        else:
            acc_scr[...] += part

    one_pass(0)
    for c in range(1, MAX_PASSES):
        pl.when(npass_ref[b * N_TOK_TILES + j] > c)(functools.partial(one_pass, c))
    o_ref[0] = x_ref[0] + mod_ref[0, 0, 5:6, :] * acc_scr[...]


def _scatter_call(win, npass, x, y, slot_t, gate_t, modsel):
    bsz = x.shape[0]
    smem = pl.BlockSpec(memory_space=pltpu.SMEM)
    tok = lambda n: pl.BlockSpec((1, TOK_TILE, n), lambda b, i: (b, i, 0))
    return pl.pallas_call(
        _scatter_kernel,
        grid=(bsz, N_TOK_TILES),
        in_specs=[smem, smem, tok(D_MODEL),
                  pl.BlockSpec((N_EXPERTS, CAP_ALL, D_MODEL), lambda b, i: (0, b, 0)),
                  tok(LANES), tok(LANES),
                  pl.BlockSpec((1, 1, 8, D_MODEL), lambda b, i: (b, jnp.minimum(i, 1), 0, 0))],
        out_specs=tok(D_MODEL),
        out_shape=jax.ShapeDtypeStruct(x.shape, F32),
        scratch_shapes=[pltpu.VMEM((TOK_TILE, D_MODEL), F32)],
        compiler_params=_params(("parallel", "arbitrary")),
        name="moe_scatter",
    )(win, npass, x, y, slot_t, gate_t, modsel)


def _rope_tables(hd, n_tile):
    rows = SEQ // GRID_W
    row = jnp.repeat(jnp.arange(rows, dtype=jnp.int32), GRID_W)
    col = jnp.tile(jnp.arange(GRID_W, dtype=jnp.int32), rows)
    half = hd // 2
    freqs = 1.0 / (ROPE_THETA ** (jnp.arange(0, half, 2, dtype=F32) / half))

    def angles(pos):
        ang = pos.astype(F32)[:, None] * freqs[None, :]
        return jnp.concatenate([ang, ang], axis=-1)

    ang = jnp.concatenate([angles(row), angles(col)], axis=-1)
    cos = jnp.concatenate([jnp.ones((CTX_LEN, hd), F32), jnp.cos(ang)], axis=0)
    sin = jnp.concatenate([jnp.zeros((CTX_LEN, hd), F32), jnp.sin(ang)], axis=0)
    return jnp.tile(cos, (1, n_tile)), jnp.tile(sin, (1, n_tile))


def _group_mean_matrix(group):
    idx = np.arange(BRANCH_W) // group
    return jnp.asarray((idx[:, None] == idx[None, :]).astype(np.float32) / group, BF16)


def _s5_tables(a_re, a_im, log_dt, b_re, b_im, c_re, c_im, bsz):
    dt = jnp.exp(log_dt.astype(F32))[..., None]
    ar, ai = a_re.astype(F32), a_im.astype(F32)
    mag = jnp.exp(ar * dt)
    lr, li = mag * jnp.cos(ai * dt), mag * jnp.sin(ai * dt)
    den = ar * ar + ai * ai
    fr = ((lr - 1.0) * ar + li * ai) / den
    fi = (li * ar - (lr - 1.0) * ai) / den
    bbr = fr[..., None] * b_re - fi[..., None] * b_im
    bbi = fr[..., None] * b_im + fi[..., None] * b_re
    eye = jnp.eye(S5_G, dtype=F32)
    bd_in = lambda m: jnp.einsum('ldgph,gk->ldghkp', m, eye).reshape(DEPTH, 2, BRANCH_W, S5_STATE)
    bmat = jnp.concatenate([bd_in(bbr), bd_in(bbi)], axis=-1).astype(BF16)
    bd_out = lambda m: jnp.einsum('ldghp,gk->ldgpkh', m, eye).reshape(DEPTH, 2, S5_STATE, BRANCH_W)
    cmat = jnp.concatenate([bd_out(c_re.astype(F32)), -bd_out(c_im.astype(F32))], axis=-2).astype(BF16)
    lam = jnp.concatenate([lr.reshape(DEPTH, 2, 1, S5_STATE), li.reshape(DEPTH, 2, 1, S5_STATE)], axis=-1)
    lam = jnp.broadcast_to(lam, (DEPTH, 2, bsz, 2 * S5_STATE))
    return bmat, lam, cmat


def _mix_weights(w_in):
    rep = B_HEADS // 2
    cut = lambda lo, n: w_in[:, :, lo:lo + n]
    dup = lambda lo: jnp.concatenate([cut(lo + (h // rep) * B_HD, B_HD) for h in range(B_HEADS)], axis=-1)
    dt_cols = jnp.pad(cut(2304, 2 * M2_HEADS), ((0, 0), (0, 0), (0, LANES - 2 * M2_HEADS)))
    w_mix = jnp.concatenate([cut(0, 1024), dup(1024), dup(1152), cut(1280, 1024), dt_cols], axis=-1)
    return w_mix.astype(BF16), w_in[:, :, 2312:].astype(BF16)


def _prepare(bsz, c, c_ctx, w_ada, b_ada, norm_g, w_in, da_q_g, da_k_g, da_lambda, da_sub_g, gqa_q_g, gqa_k_g,
             s5_a_re, s5_a_im, s5_log_dt, s5_b_re, s5_b_im, s5_c_re, s5_c_im, s5_d, s5_w_glu, s5_b_glu,
             m2_conv_w, m2_conv_b, m2_a_log, m2_dt_bias, m2_d, m2_norm_g, w_branch, w_out, w_router):
    tb = {}
    cc = jnp.concatenate([c, c_ctx[None], jnp.zeros((16 - bsz - 1, D_MODEL), F32)], axis=0)
    mod_all = _ada_call(cc, w_ada, b_ada).reshape(DEPTH, 16, 6, D_MODEL)
    mod_all = jnp.pad(mod_all, ((0, 0), (0, 0), (0, 2), (0, 0)))
    tb["modsel"] = jnp.stack([jnp.broadcast_to(mod_all[:, bsz:bsz + 1], (DEPTH, bsz, 8, D_MODEL)),
                              mod_all[:, :bsz]], axis=2)
    tb["w_mix"], tb["w_gate_in"] = _mix_weights(w_in)
    tb["cosa"], tb["sina"] = _rope_tables(A_HD, BRANCH_W // A_HD)
    tb["cosb"], tb["sinb"] = _rope_tables(B_HD, BRANCH_W // B_HD)
    tb["gmat_a"], tb["gmat_b"] = _group_mean_matrix(A_HD), _group_mean_matrix(B_HD)
    tile = lambda g: jnp.tile(g.astype(F32), (1, BRANCH_W // g.shape[-1]))
    gains = jnp.stack([tile(da_q_g), tile(da_k_g), tile(gqa_q_g), tile(gqa_k_g)], axis=1)
    tb["gains"] = jnp.pad(gains, ((0, 0), (0, 4), (0, 0)))
    amax = lambda g: jnp.max(jnp.abs(g.astype(F32)), axis=-1)
    tb["logit_bounds"] = 1.05 * jnp.stack([math.sqrt(A_HD) * amax(da_q_g) * amax(da_k_g),
                                           math.sqrt(B_HD) * amax(gqa_q_g) * amax(gqa_k_g)], axis=1).reshape(-1)
    lf = da_lambda.astype(F32)
    tb["lam_init"] = [0.8 - 0.6 * math.exp(-0.3 * l) for l in range(DEPTH)]
    tb["lam"] = (jnp.exp(jnp.sum(lf[:, 0] * lf[:, 1], axis=-1)) - jnp.exp(jnp.sum(lf[:, 2] * lf[:, 3], axis=-1))
                 + jnp.asarray(tb["lam_init"], F32)).reshape(DEPTH, 1, 1)
    row3 = lambda t: t.astype(F32).reshape(DEPTH, 1, t.shape[-1])
    tb["sub_g"] = row3(tile(da_sub_g))
    tb["s5_bmat"], tb["s5_lam"], tb["s5_cmat"] = _s5_tables(
        s5_a_re, s5_a_im, s5_log_dt, s5_b_re, s5_b_im, s5_c_re, s5_c_im, bsz)
    tb["s5_d"], tb["s5_b_glu"], tb["s5_w_glu"] = row3(s5_d), row3(s5_b_glu), s5_w_glu.astype(BF16)
    lane_row = lambda t: jnp.pad(t.astype(F32).reshape(DEPTH, 1, 2 * M2_HEADS),
                                 ((0, 0), (0, 0), (0, LANES - 2 * M2_HEADS)))
    tb["m2_a"] = lane_row(-jnp.exp(m2_a_log.astype(F32)))
    tb["m2_bias"] = lane_row(m2_dt_bias)
    tb["m2_dskip"] = row3(jnp.repeat(m2_d.astype(F32), M2_HD, axis=-1))
    tb["conv_w"] = jnp.pad(m2_conv_w.astype(F32), ((0, 0), (0, 8 - M2_CONV), (0, 0)))
    tb["conv_b"] = row3(m2_conv_b)
    tb["m2_norm_g"] = row3(m2_norm_g)
    tb["norm_rows"] = norm_g.astype(F32).reshape(DEPTH * 2, 1, D_MODEL)
    tb["w_branch"], tb["w_out"] = w_branch.astype(BF16), w_out.astype(BF16)
    tb["w_router_t"] = jnp.swapaxes(w_router.astype(F32), 1, 2)
    return tb


def _mixers(l, xs, tb):
    bsz = xs.shape[0]
    qa, ka, va, qb, kb, vb, u_tb, z, xbc, dt = _inproj_call(
        l, xs, tb["modsel"][l], tb["norm_rows"], tb["w_mix"], tb["gains"], tb["gmat_a"], tb["gmat_b"],
        tb["cosa"], tb["sina"], tb["cosb"], tb["sinb"])
    ya = _attn_call(l, tb["logit_bounds"], qa, ka, va, tb["lam"], tb["sub_g"], tb["gmat_b"], 2,
                    1.0 - tb["lam_init"][l], "diff_attn")
    yb = _attn_call(l, tb["logit_bounds"], qb, kb, vb, tb["lam"], tb["sub_g"], tb["gmat_b"], 1, None, "gqa_attn")
    u_rows = u_tb.reshape(T_ALL * bsz, BRANCH_W)
    y_s5 = _s5_call(u_rows, tb["s5_bmat"][l], tb["s5_lam"][l], tb["s5_cmat"][l], bsz)
    ys = _s5_glu_call(l, u_rows, y_s5, tb["s5_d"], tb["s5_w_glu"], tb["s5_b_glu"], bsz)
    ys = ys.reshape(T_ALL, bsz * BRANCH_W)
    ssd_args = (xbc, dt, tb["conv_w"], tb["conv_b"], tb["m2_a"], tb["m2_bias"], tb["m2_dskip"])
    y_fwd = _ssd_call(l, 0, *ssd_args)
    ym = _ssd_call(l, 1, *ssd_args, z=z, yprev=y_fwd, norm_g=tb["m2_norm_g"])
    return ya, yb, ys, ym


def _layer(l, xs, tb, w_up, w_gate, w_down):
    modsel = tb["modsel"][l]
    ya, yb, ys, ym = _mixers(l, xs, tb)
    x_mid, h2, logits_t = _merge_call(l, xs, modsel, tb["norm_rows"], ya, yb, ys, ym,
                                      tb["w_gate_in"], tb["w_branch"], tb["w_out"], tb["w_router_t"])
    slot, slot_t, gate_t, win, npass = _route_call(logits_t)
    win = win[:, :, :N_TOK_TILES].reshape(-1)
    npass = npass[:, 0, :N_TOK_TILES].reshape(-1)
    xe = _gather_call(win, npass, slot, h2)
    ye = _ffn_call(l, xe, w_up, w_gate, w_down)
    return _scatter_call(win, npass, x_mid, ye, slot_t, gate_t, modsel)


def kernel(x, c, ctx, c_ctx, w_ada, b_ada, norm_g, w_in, da_q_g, da_k_g, da_lambda, da_sub_g, gqa_q_g, gqa_k_g,
           s5_a_re, s5_a_im, s5_log_dt, s5_b_re, s5_b_im, s5_c_re, s5_c_im, s5_d, s5_w_glu, s5_b_glu,
           m2_conv_w, m2_conv_b, m2_a_log, m2_dt_bias, m2_d, m2_norm_g, w_branch, w_out, w_router,
           w_up, w_gate, w_down):
    bsz = x.shape[0]
    assert x.shape == (bsz, SEQ, D_MODEL) and ctx.shape == (bsz, CTX_LEN, D_MODEL) and bsz == 8
    tb = _prepare(bsz, c, c_ctx, w_ada, b_ada, norm_g, w_in, da_q_g, da_k_g, da_lambda, da_sub_g, gqa_q_g,
                  gqa_k_g, s5_a_re, s5_a_im, s5_log_dt, s5_b_re, s5_b_im, s5_c_re, s5_c_im, s5_d, s5_w_glu,
                  s5_b_glu, m2_conv_w, m2_conv_b, m2_a_log, m2_dt_bias, m2_d, m2_norm_g, w_branch, w_out,
                  w_router)
    xs = jnp.concatenate([ctx, x], axis=1)
    for l in range(DEPTH):
        xs = _layer(l, xs, tb, w_up, w_gate, w_down)
    return xs[:, CTX_LEN:, :]
```

```python
import functools
import math

import jax
import jax.numpy as jnp
import numpy as np
from jax import lax
from jax.experimental import pallas as pl
from jax.experimental.pallas import tpu as pltpu

F32 = jnp.float32
BF16 = jnp.bfloat16
HI = lax.Precision.HIGHEST

D_MODEL = 1024
SEQ = 2048
DEPTH = 4
GRID_W = 64
CTX_LEN = 256
T_ALL = CTX_LEN + SEQ
BRANCH_W = 256
ROPE_THETA = 10000.0
EPS = 1e-6
A_HD = 32
A_VD = 64
A_HEADS = 4
B_HEADS = 4
B_HD = 64
S5_H = 16
S5_G = 16
S5_P = 64
S5_STATE = S5_G * S5_P
M2_HD = 64
M2_HEADS = 4
M2_GROUPS = 2
M2_N = 64
M2_CONV = 5
M2_XBC = 512
N_EXPERTS = 16
EXPERT_FF = 2 * D_MODEL
EC_FACTOR = 2
CAP_CTX = EC_FACTOR * CTX_LEN // N_EXPERTS
CAP_LAT = EC_FACTOR * SEQ // N_EXPERTS
CAP_ALL = CAP_CTX + CAP_LAT

LANES = 128
TOK_TILE = 256
N_TOK_TILES = T_ALL // TOK_TILE
CHUNK = 128
N_CHUNKS = T_ALL // CHUNK
CTX_CHUNKS = CTX_LEN // CHUNK
MIX_COLS = 2944
FF_TILE = 512
VMEM_LIMIT = 56 * 1024 * 1024
LOG2E = 1.0 / math.log(2.0)
SAFE_LOGIT = 40.0


def _params(sem, vmem=VMEM_LIMIT):
    return pltpu.CompilerParams(dimension_semantics=sem, vmem_limit_bytes=vmem)


def _sigmoid(x):
    return 1.0 / (1.0 + jnp.exp(-x))


def _silu(x):
    return x * _sigmoid(x)


def _rms_rows(x):
    return x * lax.rsqrt(jnp.mean(x * x, axis=-1, keepdims=True) + EPS)


def _ada_kernel(c_ref, w_ref, b_ref, o_ref):
    sc = _silu(c_ref[...])
    o_ref[0] = jnp.dot(sc, w_ref[0], precision=HI, preferred_element_type=F32) + b_ref[0]


def _ada_call(cc, w_ada, b_ada):
    tn = 1536
    return pl.pallas_call(
        _ada_kernel,
        grid=(DEPTH, 6 * D_MODEL // tn),
        in_specs=[
            pl.BlockSpec((16, D_MODEL), lambda l, j: (0, 0)),
            pl.BlockSpec((1, D_MODEL, tn), lambda l, j: (l, 0, j)),
            pl.BlockSpec((1, 1, tn), lambda l, j: (l, 0, j)),
        ],
        out_specs=pl.BlockSpec((1, 16, tn), lambda l, j: (l, 0, j)),
        out_shape=jax.ShapeDtypeStruct((DEPTH, 16, 6 * D_MODEL), F32),
        compiler_params=_params(("arbitrary", "arbitrary")),
        name="ada_mod",
    )(cc, w_ada, b_ada.reshape(DEPTH, 1, 6 * D_MODEL))


def _group_mean(x, gmat):
    hi = x.astype(BF16)
    lo = (x - hi.astype(F32)).astype(BF16)
    return jnp.dot(hi, gmat, preferred_element_type=F32) + jnp.dot(lo, gmat, preferred_element_type=F32)


def _group_norm_rope(x, gmat, gain, cos, sin, shift):
    xn = x * lax.rsqrt(_group_mean(x * x, gmat) + EPS) * gain
    lane = lax.broadcasted_iota(jnp.int32, xn.shape, 1)
    up = pltpu.roll(xn, BRANCH_W - shift, 1)
    dn = pltpu.roll(xn, shift, 1)
    rot = jnp.where((lane % (2 * shift)) < shift, -up, dn)
    return xn * cos + rot * sin


def _inproj_kernel(x_ref, mod_ref, g_ref, w_ref, gains_ref, ga_ref, gb_ref,
                   cosa_ref, sina_ref, cosb_ref, sinb_ref,
                   qa_ref, ka_ref, va_ref, qb_ref, kb_ref, vb_ref, u_ref, z_ref, xbc_ref, dt_ref):
    mod = mod_ref[0, 0]
    h = _rms_rows(x_ref[0]) * g_ref[0] * (1.0 + mod[1:2]) + mod[0:1]
    p = jnp.dot(h.astype(BF16), w_ref[0], preferred_element_type=F32)
    gains = gains_ref[0]
    w = BRANCH_W
    qa = _group_norm_rope(p[:, 0:w], ga_ref[...], gains[0:1], cosa_ref[...], sina_ref[...], A_HD // 4)
    ka = _group_norm_rope(p[:, w:2 * w], ga_ref[...], gains[1:2], cosa_ref[...], sina_ref[...], A_HD // 4)
    qb = _group_norm_rope(p[:, 3 * w:4 * w], gb_ref[...], gains[2:3], cosb_ref[...], sinb_ref[...], B_HD // 4)
    kb = _group_norm_rope(p[:, 4 * w:5 * w], gb_ref[...], gains[3:4], cosb_ref[...], sinb_ref[...], B_HD // 4)
    qa_ref[0] = (qa * (LOG2E / math.sqrt(A_HD))).astype(BF16)
    ka_ref[0] = ka.astype(BF16)
    va_ref[0] = p[:, 2 * w:3 * w].astype(BF16)
    qb_ref[0] = (qb * (LOG2E / math.sqrt(B_HD))).astype(BF16)
    kb_ref[0] = kb.astype(BF16)
    vb_ref[0] = p[:, 5 * w:6 * w].astype(BF16)
    u_ref[0] = p[:, 6 * w:7 * w].astype(BF16)
    z_ref[0] = p[:, 7 * w:8 * w]
    xbc_ref[0] = p[:, 8 * w:10 * w]
    dt_ref[0] = p[:, 10 * w:10 * w + LANES]


def _inproj_call(l, x, modsel, norm_g, w_mix, gains, gmat_a, gmat_b, cosa, sina, cosb, sinb):
    bsz = x.shape[0]
    w = BRANCH_W
    tok = lambda n: pl.BlockSpec((1, TOK_TILE, n), lambda b, i: (b, i, 0))
    tab = pl.BlockSpec((TOK_TILE, w), lambda b, i: (i, 0))
    full2 = lambda a: pl.BlockSpec(a.shape, lambda b, i: (0, 0))
    out_bt = lambda n, dt: jax.ShapeDtypeStruct((bsz, T_ALL, n), dt)
    return pl.pallas_call(
        _inproj_kernel,
        grid=(bsz, N_TOK_TILES),
        in_specs=[
            tok(D_MODEL),
            pl.BlockSpec((1, 1, 8, D_MODEL), lambda b, i: (b, jnp.minimum(i, 1), 0, 0)),
            pl.BlockSpec((1, 1, D_MODEL), lambda b, i: (2 * l, 0, 0)),
            pl.BlockSpec((1, D_MODEL, MIX_COLS), lambda b, i: (l, 0, 0)),
            pl.BlockSpec((1, 8, w), lambda b, i: (l, 0, 0)),
            full2(gmat_a), full2(gmat_b), tab, tab, tab, tab,
        ],
        out_specs=[tok(w)] * 8 + [tok(M2_XBC), tok(LANES)],
        out_shape=[out_bt(w, BF16)] * 7 + [out_bt(w, F32), out_bt(M2_XBC, F32), out_bt(LANES, F32)],
        compiler_params=_params(("parallel", "parallel")),
        name="in_proj",
    )(x, modsel, norm_g, w_mix, gains, gmat_a, gmat_b, cosa, sina, cosb, sinb)


def _attn_kernel(bound_ref, q_ref, k_ref, v_ref, lam_ref, subg_ref, gmat_ref, o_ref, *,
                 n_maps, post_scale, bound_idx):
    lane = lax.broadcasted_iota(jnp.int32, (TOK_TILE, BRANCH_W), 1)
    q = q_ref[0]
    map_w = A_VD // n_maps

    def attend(n_keys, shift):
        k = k_ref[0, 0:n_keys, :]
        v = v_ref[0, 0:n_keys, :]
        out = jnp.zeros((TOK_TILE, BRANCH_W), F32)
        for head in range(A_HEADS):
            acc = None
            for m in range(n_maps):
                lo = head * A_VD + m * map_w
                qm = jnp.where((lane >= lo) & (lane < lo + map_w), q, jnp.zeros_like(q))
                s = lax.dot_general(qm, k, (((1,), (1,)), ((), ())), preferred_element_type=F32)
                if shift:
                    s = s - jnp.max(s, axis=-1, keepdims=True)
                e = jnp.exp2(s)
                r = 1.0 / jnp.sum(e, axis=-1, keepdims=True)
                o = jnp.dot(e.astype(BF16), v, preferred_element_type=F32)
                acc = o * r if m == 0 else acc - o * (r * lam_ref[0])
            out = jnp.where((lane >= head * A_VD) & (lane < (head + 1) * A_VD), acc, out)
        if post_scale is not None:
            out = out * lax.rsqrt(_group_mean(out * out, gmat_ref[...]) + EPS) * subg_ref[0] * post_scale
        o_ref[0] = out.astype(BF16)

    is_ctx = pl.program_id(1) == 0
    small = bound_ref[bound_idx] <= SAFE_LOGIT
    for n_keys, tile_sel in ((CTX_LEN, is_ctx), (T_ALL, jnp.logical_not(is_ctx))):
        for shift, bound_sel in ((False, small), (True, jnp.logical_not(small))):
            pl.when(jnp.logical_and(tile_sel, bound_sel))(functools.partial(attend, n_keys, shift))


def _attn_call(l, bounds, q, k, v, lam, subg, gmat, n_maps, post_scale, name):
    bsz = q.shape[0]
    w = BRANCH_W
    return pl.pallas_call(
        functools.partial(_attn_kernel, n_maps=n_maps, post_scale=post_scale,
                          bound_idx=2 * l + (0 if n_maps == 2 else 1)),
        grid=(bsz, N_TOK_TILES),
        in_specs=[
            pl.BlockSpec(memory_space=pltpu.SMEM),
            pl.BlockSpec((1, TOK_TILE, w), lambda b, i: (b, i, 0)),
            pl.BlockSpec((1, T_ALL, w), lambda b, i: (b, 0, 0)),
            pl.BlockSpec((1, T_ALL, w), lambda b, i: (b, 0, 0)),
            pl.BlockSpec((1, 1, 1), lambda b, i: (l, 0, 0)),
            pl.BlockSpec((1, 1, w), lambda b, i: (l, 0, 0)),
            pl.BlockSpec(gmat.shape, lambda b, i: (0, 0)),
        ],
        out_specs=pl.BlockSpec((1, TOK_TILE, w), lambda b, i: (b, i, 0)),
        out_shape=jax.ShapeDtypeStruct((bsz, T_ALL, w), BF16),
        compiler_params=_params(("parallel", "parallel")),
        name=name,
    )(bounds, q, k, v, lam, subg, gmat)


def _scan_chunk_index(direction, j):
    back = jnp.where(j < CTX_CHUNKS, CTX_CHUNKS - 1 - j, N_CHUNKS + CTX_CHUNKS - 1 - j)
    return jnp.where(direction == 0, j, back)


S5_PITCH = CHUNK + 8
N_SLABS = 2 * S5_STATE // LANES


def _s5_kernel(u_ref, bmat_ref, lam_ref, cmat_ref, y_ref, bu_scr, h_scr, *, bsz):
    direction = pl.program_id(0)

    @pl.when(pl.program_id(1) == 0)
    def _():
        h_scr[...] = jnp.zeros_like(h_scr)

    for b in range(bsz):
        bu = jnp.dot(u_ref[b], bmat_ref[0], preferred_element_type=F32)
        for s in range(N_SLABS):
            bu_scr[s, b * S5_PITCH:b * S5_PITCH + CHUNK, :] = bu[:, s * LANES:(s + 1) * LANES]
    lam_re = lam_ref[0, :, 0:S5_STATE]
    lam_im = lam_ref[0, :, S5_STATE:2 * S5_STATE]

    def step(t, carry):
        h_re, h_im = carry
        tt = jnp.where(direction == 0, t, CHUNK - 1 - t)
        rows = pl.ds(tt, bsz, stride=S5_PITCH)
        bu = jnp.concatenate([bu_scr[s, rows, :] for s in range(N_SLABS)], axis=1)
        n_re = lam_re * h_re - lam_im * h_im + bu[:, 0:S5_STATE]
        n_im = lam_re * h_im + lam_im * h_re + bu[:, S5_STATE:2 * S5_STATE]
        for s in range(N_SLABS // 2):
            bu_scr[s, rows, :] = n_re[:, s * LANES:(s + 1) * LANES]
            bu_scr[N_SLABS // 2 + s, rows, :] = n_im[:, s * LANES:(s + 1) * LANES]
        return n_re, n_im

    h_re, h_im = lax.fori_loop(0, CHUNK, step, (h_scr[:, 0:S5_STATE], h_scr[:, S5_STATE:2 * S5_STATE]),
                               unroll=4)
    h_scr[:, 0:S5_STATE] = h_re
    h_scr[:, S5_STATE:2 * S5_STATE] = h_im
    for b in range(bsz):
        states = jnp.concatenate([bu_scr[s, b * S5_PITCH:b * S5_PITCH + CHUNK, :] for s in range(N_SLABS)],
                                 axis=1)
        y_ref[0, b] = jnp.dot(states.astype(BF16), cmat_ref[0], preferred_element_type=F32)


def _s5_call(u, bmat, lam, cmat):
    bsz = u.shape[0]
    return pl.pallas_call(
        functools.partial(_s5_kernel, bsz=bsz),
        grid=(2, N_CHUNKS),
        in_specs=[
            pl.BlockSpec((bsz, CHUNK, BRANCH_W), lambda d, j: (0, _scan_chunk_index(d, j), 0)),
            pl.BlockSpec((1, BRANCH_W, 2 * S5_STATE), lambda d, j: (d, 0, 0)),
            pl.BlockSpec((1, bsz, 2 * S5_STATE), lambda d, j: (d, 0, 0)),
            pl.BlockSpec((1, 2 * S5_STATE, BRANCH_W), lambda d, j: (d, 0, 0)),
        ],
        out_specs=pl.BlockSpec((1, bsz, CHUNK, BRANCH_W), lambda d, j: (d, 0, _scan_chunk_index(d, j), 0)),
        out_shape=jax.ShapeDtypeStruct((2, bsz, T_ALL, BRANCH_W), F32),
        scratch_shapes=[pltpu.VMEM((N_SLABS, bsz * S5_PITCH, LANES), F32), pltpu.VMEM((bsz, 2 * S5_STATE), F32)],
        compiler_params=_params(("arbitrary", "arbitrary")),
        name="s5_scan",
    )(u, bmat, lam, cmat)


def _s5_glu_kernel(u_ref, y_ref, d_ref, w_ref, b_ref, o_ref):
    t = d_ref[0] * u_ref[0].astype(F32) + y_ref[0, 0] + y_ref[1, 0]
    t = 0.5 * t * (1.0 + jnp.tanh(math.sqrt(2.0 / math.pi) * (t + 0.044715 * (t * t * t))))
    gate = jnp.dot(t.astype(BF16), w_ref[0], preferred_element_type=F32) + b_ref[0]
    o_ref[0] = (t * _sigmoid(gate)).astype(BF16)


def _s5_glu_call(l, u, y, s5_d, w_glu, b_glu):
    bsz = u.shape[0]
    w = BRANCH_W
    tok = pl.BlockSpec((1, TOK_TILE, w), lambda b, i: (b, i, 0))
    return pl.pallas_call(
        _s5_glu_kernel,
        grid=(bsz, N_TOK_TILES),
        in_specs=[
            tok,
            pl.BlockSpec((2, 1, TOK_TILE, w), lambda b, i: (0, b, i, 0)),
            pl.BlockSpec((1, 1, w), lambda b, i: (l, 0, 0)),
            pl.BlockSpec((1, w, w), lambda b, i: (l, 0, 0)),
            pl.BlockSpec((1, 1, w), lambda b, i: (l, 0, 0)),
        ],
        out_specs=tok,
        out_shape=jax.ShapeDtypeStruct((bsz, T_ALL, w), BF16),
        compiler_params=_params(("parallel", "parallel")),
        name="s5_glu",
    )(u, y, s5_d, w_glu, b_glu)


PAD_ROWS = 8


def _softplus(x):
    return jnp.maximum(x, 0.0) + jnp.log(1.0 + jnp.exp(-jnp.abs(x)))


def _ssd_kernel(*refs, direction, final):
    if final:
        (xbc_ref, dt_ref, cw_ref, cb_ref, a_ref, bias_ref, dskip_ref, z_ref, yprev_ref, ng_ref,
         o_ref, pad_scr, xact_scr, dtv_scr, ad_scr, st_scr) = refs
    else:
        (xbc_ref, dt_ref, cw_ref, cb_ref, a_ref, bias_ref, dskip_ref,
         o_ref, pad_scr, xact_scr, dtv_scr, ad_scr, st_scr) = refs
    rev = direction == 1

    lat0 = CTX_LEN + 3 * PAD_ROWS - PAD_ROWS
    pad_scr[...] = jnp.zeros_like(pad_scr)
    pad_scr[PAD_ROWS:PAD_ROWS + CTX_LEN, :] = xbc_ref[0, 0:CTX_LEN, :]
    pad_scr[lat0:lat0 + SEQ, :] = xbc_ref[0, CTX_LEN:T_ALL, :]
    for c in range(N_CHUNKS):
        base = c * CHUNK + (PAD_ROWS if c < CTX_CHUNKS else lat0 - CTX_LEN)
        acc = jnp.broadcast_to(cb_ref[0], (CHUNK, M2_XBC))
        for tap in range(M2_CONV):
            start = base + tap - M2_CONV // 2
            acc = acc + cw_ref[0, tap:tap + 1, :] * pad_scr[start:start + CHUNK, :]
        xact_scr[c * CHUNK:(c + 1) * CHUNK, :] = _silu(acc)

    dtv = _softplus(dt_ref[0] + bias_ref[0])
    dtv_scr[...] = dtv
    ad_scr[...] = dtv * a_ref[0]
    st_scr[...] = jnp.zeros_like(st_scr)

    ti = lax.broadcasted_iota(jnp.int32, (CHUNK, CHUNK), 0)
    si = lax.broadcasted_iota(jnp.int32, (CHUNK, CHUNK), 1)
    tri = jnp.where(ti >= si, 1.0, 0.0).astype(F32)
    keep = (si >= ti) if rev else (ti >= si)

    def chunk_body(j, carry):
        ci = _scan_chunk_index(direction, j)
        rows = pl.ds(pl.multiple_of(ci * CHUNK, CHUNK), CHUNK)
        xa = xact_scr[rows, :]
        dtc = dtv_scr[rows, :]
        adc = ad_scr[rows, :]
        cs = jnp.dot(tri, adc, precision=HI, preferred_element_type=F32)
        tot = cs[CHUNK - 1:CHUNK, :]
        pcs = cs - adc if rev else cs
        pcs_t = jnp.transpose(pcs)
        ys = []
        for grp in range(M2_GROUPS):
            b0 = BRANCH_W + grp * M2_N
            c0 = BRANCH_W + M2_GROUPS * M2_N + grp * M2_N
            bg = xa[:, b0:b0 + M2_N]
            cg = xa[:, c0:c0 + M2_N].astype(BF16)
            gmat = lax.dot_general(cg, bg.astype(BF16), (((1,), (1,)), ((), ())), preferred_element_type=F32)
            for hh in range(M2_HEADS // M2_GROUPS):
                head = grp * (M2_HEADS // M2_GROUPS) + hh
                ch = direction * M2_HEADS + head
                col = pcs[:, ch:ch + 1]
                row = pcs_t[ch:ch + 1, :]
                tot_h = tot[:, ch:ch + 1]
                arg = (row - col) if rev else (col - row)
                lmat = jnp.exp(jnp.where(keep, arg, -jnp.inf))
                xd = (xa[:, head * M2_HD:(head + 1) * M2_HD] * dtc[:, ch:ch + 1]).astype(BF16)
                y_diag = jnp.dot((gmat * lmat).astype(BF16), xd, preferred_element_type=F32)
                off = jnp.exp(tot_h - col) if rev else jnp.exp(col)
                dte = jnp.exp(col) if rev else jnp.exp(tot_h - col)
                state = st_scr[head]
                y_off = jnp.dot(cg, state.astype(BF16), preferred_element_type=F32) * off
                st_scr[head] = jnp.exp(tot_h) * state + lax.dot_general(
                    (bg * dte).astype(BF16), xd, (((0,), (0,)), ((), ())), preferred_element_type=F32)
                ys.append(y_diag + y_off)
        y = jnp.concatenate(ys, axis=-1)
        if final:
            y = y + yprev_ref[0, rows, :]
            g = y * _silu(z_ref[0, rows, :])
            o_ref[0, rows, :] = (_rms_rows(g) * ng_ref[0]).astype(BF16)
        else:
            o_ref[0, rows, :] = y + dskip_ref[0] * xa[:, 0:BRANCH_W]
        return carry

    lax.fori_loop(0, N_CHUNKS, chunk_body, 0)


def _ssd_call(l, direction, xbc, dt, conv_w, conv_b, a_rows, bias_rows, dskip, z=None, yprev=None, norm_g=None):
    bsz = xbc.shape[0]
    final = z is not None
    w = BRANCH_W
    seq = lambda n: pl.BlockSpec((1, T_ALL, n), lambda b: (b, 0, 0))
    lay = lambda r, n: pl.BlockSpec((1, r, n), lambda b: (l, 0, 0))
    in_specs = [seq(M2_XBC), seq(LANES), lay(8, M2_XBC), lay(1, M2_XBC), lay(1, LANES), lay(1, LANES), lay(1, w)]
    args = [xbc, dt, conv_w, conv_b, a_rows, bias_rows, dskip]
    if final:
        in_specs += [seq(w), seq(w), lay(1, w)]
        args += [z, yprev, norm_g]
    return pl.pallas_call(
        functools.partial(_ssd_kernel, direction=direction, final=final),
        grid=(bsz,),
        in_specs=in_specs,
        out_specs=seq(w),
        out_shape=jax.ShapeDtypeStruct((bsz, T_ALL, w), BF16 if final else F32),
        scratch_shapes=[
            pltpu.VMEM((T_ALL + 3 * PAD_ROWS, M2_XBC), F32),
            pltpu.VMEM((T_ALL, M2_XBC), F32),
            pltpu.VMEM((T_ALL, LANES), F32),
            pltpu.VMEM((T_ALL, LANES), F32),
            pltpu.VMEM((M2_HEADS, M2_N, M2_HD), F32),
        ],
        compiler_params=_params(("parallel",)),
        name="ssd_bwd" if final else "ssd_fwd",
    )(*args)


MERGE_TILE = 768


def _merge_kernel(x_ref, mod_ref, g_ref, ya_ref, yb_ref, ys_ref, ym_ref, wg_ref, wb_ref, wo_ref, wr_ref,
                  xo_ref, h2_ref, lg_ref):
    row = pl.program_id(1) * MERGE_TILE + lax.broadcasted_iota(jnp.int32, (MERGE_TILE, 1), 0)
    is_ctx = row < CTX_LEN
    mod = lambda k: jnp.where(is_ctx, mod_ref[0, 0, k:k + 1, :], mod_ref[0, 1, k:k + 1, :])
    x = x_ref[0]
    h = (_rms_rows(x) * g_ref[0] * (1.0 + mod(1)) + mod(0)).astype(BF16)
    acc = jnp.zeros((MERGE_TILE, D_MODEL), F32)
    for n, y_ref in enumerate((ya_ref, yb_ref, ys_ref, ym_ref)):
        yv = y_ref[0]
        gate = _sigmoid(jnp.dot(h, wg_ref[0, :, n * D_MODEL:(n + 1) * D_MODEL], preferred_element_type=F32))
        acc = acc + gate * jnp.dot(yv, wb_ref[0, n], preferred_element_type=F32)
    xn = x + mod(2) * jnp.dot(acc.astype(BF16), wo_ref[0], preferred_element_type=F32)
    xo_ref[0] = xn
    h2 = _rms_rows(xn) * g_ref[1] * (1.0 + mod(4)) + mod(3)
    h2_ref[0] = h2.astype(BF16)
    lg_ref[0] = lax.dot_general(wr_ref[0], h2, (((1,), (1,)), ((), ())), precision=HI, preferred_element_type=F32)


def _merge_call(l, x, modsel, norm_g, ya, yb, ys, ym, w_gate_in, w_branch, w_out, w_router_t):
    bsz = x.shape[0]
    w = BRANCH_W
    tok = lambda n: pl.BlockSpec((1, MERGE_TILE, n), lambda b, i: (b, i, 0))
    return pl.pallas_call(
        _merge_kernel,
        grid=(bsz, T_ALL // MERGE_TILE),
        in_specs=[
            tok(D_MODEL),
            pl.BlockSpec((1, 2, 8, D_MODEL), lambda b, i: (b, 0, 0, 0)),
            pl.BlockSpec((2, 1, D_MODEL), lambda b, i: (l, 0, 0)),
            tok(w), tok(w), tok(w), tok(w),
            pl.BlockSpec((1, D_MODEL, 4 * D_MODEL), lambda b, i: (l, 0, 0)),
            pl.BlockSpec((1, 4, w, D_MODEL), lambda b, i: (l, 0, 0, 0)),
            pl.BlockSpec((1, D_MODEL, D_MODEL), lambda b, i: (l, 0, 0)),
            pl.BlockSpec((1, N_EXPERTS, D_MODEL), lambda b, i: (l, 0, 0)),
        ],
        out_specs=[tok(D_MODEL), tok(D_MODEL), pl.BlockSpec((1, N_EXPERTS, MERGE_TILE), lambda b, i: (b, 0, i))],
        out_shape=[jax.ShapeDtypeStruct((bsz, T_ALL, D_MODEL), F32),
                   jax.ShapeDtypeStruct((bsz, T_ALL, D_MODEL), BF16),
                   jax.ShapeDtypeStruct((bsz, N_EXPERTS, T_ALL), F32)],
        compiler_params=_params(("parallel", "parallel")),
        name="merge",
    )(x, modsel, norm_g, ya, yb, ys, ym, w_gate_in, w_branch, w_out, w_router_t)


BISECT_STEPS = 48
SLOT_WIN = 64
SLOT_ALIGN = 16
MAX_PASSES = -(-(CAP_LAT + SLOT_ALIGN - 1) // SLOT_WIN)


def _route_kernel(lg_ref, slot_ref, slot_t_ref, gate_t_ref, win_ref, npass_ref):
    lg = lg_ref[0]
    sh = lg - jnp.max(lg, axis=0, keepdims=True)
    ex = jnp.exp(sh)
    den = jnp.sum(ex, axis=0, keepdims=True)
    aff = ex / den
    logaff = sh - jnp.log(den)
    ri = lax.broadcasted_iota(jnp.int32, (LANES, LANES), 0)
    ci = lax.broadcasted_iota(jnp.int32, (LANES, LANES), 1)
    upper = jnp.where(ri <= ci, 1.0, 0.0).astype(BF16)

    def prefix_exclusive(mask, lo, hi):
        carry = jnp.zeros((N_EXPERTS, 1), F32)
        parts = []
        for blk in range((hi - lo) // LANES):
            m = mask[:, blk * LANES:(blk + 1) * LANES]
            inc = jnp.dot(m.astype(BF16), upper, preferred_element_type=F32)
            parts.append(inc - m + carry)
            carry = carry + inc[:, LANES - 1:LANES]
        return jnp.concatenate(parts, axis=1)

    segments = ((0, CTX_LEN, CAP_CTX, 0), (CTX_LEN, T_ALL, CAP_LAT, CAP_CTX))

    def bisect(_, bounds):
        out = []
        for (lo, hi, cap, _), (low, high) in zip(segments, bounds):
            mid = 0.5 * (low + high)
            cnt = jnp.sum(jnp.where(logaff[:, lo:hi] >= mid, 1.0, 0.0), axis=1, keepdims=True)
            ok = cnt >= cap
            out.append((jnp.where(ok, mid, low), jnp.where(ok, high, mid)))
        return tuple(out)

    start = tuple((jnp.min(logaff[:, lo:hi], axis=1, keepdims=True), jnp.ones((N_EXPERTS, 1), F32))
                  for lo, hi, _, _ in segments)
    bounds = lax.fori_loop(0, BISECT_STEPS, bisect, start)

    def choose(lo, hi, cap, base, low, high):
        seg = logaff[:, lo:hi]
        gt = jnp.where(seg >= high, 1.0, 0.0)
        eq = jnp.where(seg >= low, 1.0, 0.0) - gt
        need = cap - jnp.sum(gt, axis=1, keepdims=True)
        sel = gt + eq * jnp.where(prefix_exclusive(eq, lo, hi) < need, 1.0, 0.0)
        pos = prefix_exclusive(sel, lo, hi) + base
        slot = jnp.where(sel > 0.0, pos, -1.0)
        gate = sel * aff[:, lo:hi]
        slot_ref[0, :, lo:hi] = slot.astype(jnp.int32)
        fill = jnp.zeros((LANES - N_EXPERTS, LANES), F32)
        for blk in range((hi - lo) // LANES):
            cols = slice(blk * LANES, (blk + 1) * LANES)
            rows = slice(lo + blk * LANES, lo + (blk + 1) * LANES)
            slot_t_ref[0, rows, :] = jnp.transpose(jnp.concatenate([slot[:, cols], fill], axis=0))
            gate_t_ref[0, rows, :] = jnp.transpose(jnp.concatenate([gate[:, cols], fill], axis=0))
        tiles = []
        for t in range((hi - lo) // TOK_TILE):
            cols = slice(t * TOK_TILE, (t + 1) * TOK_TILE)
            chosen = sel[:, cols] > 0.0
            first = jnp.min(jnp.where(chosen, pos[:, cols], float(CAP_ALL)), axis=1, keepdims=True)
            last = jnp.max(jnp.where(chosen, pos[:, cols], -1.0), axis=1, keepdims=True)
            start = jnp.minimum(jnp.floor(first * (1.0 / SLOT_ALIGN)) * SLOT_ALIGN, float(CAP_ALL - SLOT_WIN))
            tiles.append((lo // TOK_TILE + t, start, jnp.floor((last - start + SLOT_WIN) * (1.0 / SLOT_WIN))))
        return tiles

    lane = lax.broadcasted_iota(jnp.int32, (N_EXPERTS, LANES), 1)
    win = jnp.zeros((N_EXPERTS, LANES), F32)
    npass = jnp.zeros((N_EXPERTS, LANES), F32)
    for seg_def, (low, high) in zip(segments, bounds):
        for j, start, passes in choose(*seg_def, low, high):
            win = jnp.where(lane == j, start, win)
            npass = jnp.where(lane == j, passes, npass)
    win_ref[0] = win.astype(jnp.int32)
    npass_ref[0] = jnp.broadcast_to(jnp.max(npass, axis=0, keepdims=True), (8, LANES)).astype(jnp.int32)


def _route_call(logits_t):
    bsz = logits_t.shape[0]
    spec = pl.BlockSpec((1, N_EXPERTS, T_ALL), lambda b: (b, 0, 0))
    spec_t = pl.BlockSpec((1, T_ALL, LANES), lambda b: (b, 0, 0))
    return pl.pallas_call(
        _route_kernel,
        grid=(bsz,),
        in_specs=[spec],
        out_specs=[spec, spec_t, spec_t, pl.BlockSpec((1, N_EXPERTS, LANES), lambda b: (b, 0, 0)),
                   pl.BlockSpec((1, 8, LANES), lambda b: (b, 0, 0))],
        out_shape=[jax.ShapeDtypeStruct((bsz, N_EXPERTS, T_ALL), jnp.int32),
                   jax.ShapeDtypeStruct((bsz, T_ALL, LANES), F32),
                   jax.ShapeDtypeStruct((bsz, T_ALL, LANES), F32),
                   jax.ShapeDtypeStruct((bsz, N_EXPERTS, LANES), jnp.int32),
                   jax.ShapeDtypeStruct((bsz, 8, LANES), jnp.int32)],
        compiler_params=_params(("parallel",)),
        name="route",
    )(logits_t)


def _window(win_ref, b, j, e, c):
    low = win_ref[(b * N_EXPERTS + e) * N_TOK_TILES + j] + c * SLOT_WIN
    return pl.multiple_of(jnp.minimum(low, CAP_ALL - SLOT_WIN), SLOT_ALIGN), low


def _gather_kernel(win_ref, npass_ref, slot_ref, h_ref, xs_ref):
    b, j = pl.program_id(0), pl.program_id(1)

    @pl.when(j == 0)
    def _():
        xs_ref[...] = jnp.zeros_like(xs_ref)

    def one_pass(c):
        rows = lax.broadcasted_iota(jnp.int32, (SLOT_WIN, TOK_TILE), 0)
        starts, blocks = [], []
        for e in range(N_EXPERTS):
            start, low = _window(win_ref, b, j, e, c)
            ids = rows + start
            hit = jnp.where(slot_ref[0, e:e + 1, :] == ids, jnp.where(ids >= low, 1.0, 0.0), 0.0)
            starts.append(start)
            blocks.append(hit.astype(BF16))
        picked = jnp.dot(jnp.concatenate(blocks, axis=0), h_ref[0], preferred_element_type=F32)
        for e, start in enumerate(starts):
            xs_ref[e, pl.ds(start, SLOT_WIN), :] += picked[e * SLOT_WIN:(e + 1) * SLOT_WIN].astype(BF16)

    one_pass(0)
    for c in range(1, MAX_PASSES):
        pl.when(npass_ref[b * N_TOK_TILES + j] > c)(functools.partial(one_pass, c))


def _gather_call(win, npass, slot, h2):
    bsz = slot.shape[0]
    smem = pl.BlockSpec(memory_space=pltpu.SMEM)
    return pl.pallas_call(
        _gather_kernel,
        grid=(bsz, N_TOK_TILES),
        in_specs=[smem, smem,
                  pl.BlockSpec((1, N_EXPERTS, TOK_TILE), lambda b, j: (b, 0, j)),
                  pl.BlockSpec((1, TOK_TILE, D_MODEL), lambda b, j: (b, j, 0))],
        out_specs=pl.BlockSpec((N_EXPERTS, CAP_ALL, D_MODEL), lambda b, j: (0, b, 0)),
        out_shape=jax.ShapeDtypeStruct((N_EXPERTS, bsz * CAP_ALL, D_MODEL), BF16),
        compiler_params=_params(("parallel", "arbitrary")),
        name="moe_gather",
    )(win, npass, slot, h2)


def _ffn_kernel(x_ref, wu_ref, wg_ref, wd_ref, y_ref, acc_scr, *, row_tile):
    f = pl.program_id(1)
    wu = wu_ref[0, 0].astype(BF16)
    wg = wg_ref[0, 0].astype(BF16)
    wd = wd_ref[0, 0].astype(BF16)
    n_rows = x_ref.shape[1]

    @pl.when(f == 0)
    def _():
        acc_scr[...] = jnp.zeros_like(acc_scr)

    for r0 in range(0, n_rows, row_tile):
        x = x_ref[0, r0:r0 + row_tile, :]
        up = jnp.dot(x, wu, preferred_element_type=F32)
        gt = jnp.dot(x, wg, preferred_element_type=F32)
        acc_scr[r0:r0 + row_tile, :] += jnp.dot((_silu(gt) * up).astype(BF16), wd, preferred_element_type=F32)

    @pl.when(f == EXPERT_FF // FF_TILE - 1)
    def _():
        y_ref[0] = acc_scr[...].astype(BF16)


def _ffn_call(l, xs, w_up, w_gate, w_down):
    n_rows = xs.shape[1]
    row_tile = n_rows // 4
    return pl.pallas_call(
        functools.partial(_ffn_kernel, row_tile=row_tile),
        grid=(N_EXPERTS, EXPERT_FF // FF_TILE),
        in_specs=[
            pl.BlockSpec((1, n_rows, D_MODEL), lambda e, f: (e, 0, 0)),
            pl.BlockSpec((1, 1, D_MODEL, FF_TILE), lambda e, f: (l, e, 0, f)),
            pl.BlockSpec((1, 1, D_MODEL, FF_TILE), lambda e, f: (l, e, 0, f)),
            pl.BlockSpec((1, 1, FF_TILE, D_MODEL), lambda e, f: (l, e, f, 0)),
        ],
        out_specs=pl.BlockSpec((1, n_rows, D_MODEL), lambda e, f: (e, 0, 0)),
        out_shape=jax.ShapeDtypeStruct(xs.shape, BF16),
        scratch_shapes=[pltpu.VMEM((n_rows, D_MODEL), F32)],
        compiler_params=_params(("parallel", "arbitrary")),
        name="moe_ffn",
    )(xs, w_up, w_gate, w_down)


def _scatter_kernel(win_ref, npass_ref, x_ref, y_ref, slot_ref, gate_ref, mod_ref, o_ref, acc_scr, *, first_tile):
    b, j = pl.program_id(0), pl.program_id(1) + first_tile
    slot_t = slot_ref[0]
    gate_t = gate_ref[0].astype(BF16)
    width = N_EXPERTS * SLOT_WIN
    spread = jnp.where(lax.broadcasted_iota(jnp.int32, (LANES, width), 1) // SLOT_WIN
                       == lax.broadcasted_iota(jnp.int32, (LANES, width), 0), 1.0, 0.0).astype(BF16)
    gate_w = jnp.dot(gate_t, spread, preferred_element_type=F32)
    offset = (lax.broadcasted_iota(jnp.int32, (TOK_TILE, width), 1) % SLOT_WIN).astype(F32)
    lane = lax.broadcasted_iota(jnp.int32, (1, LANES), 1)

    def one_pass(c):
        start_row = jnp.zeros((1, LANES), F32)
        low_row = jnp.zeros((1, LANES), F32)
        rows = []
        for e in range(N_EXPERTS):
            start, low = _window(win_ref, b, j, e, c)
            start_row = jnp.where(lane == e, start.astype(F32), start_row)
            low_row = jnp.where(lane == e, low.astype(F32), low_row)
            rows.append(y_ref[e, pl.ds(start, SLOT_WIN), :])
        rel = jnp.where(slot_t >= low_row, jnp.minimum(slot_t - start_row, float(SLOT_WIN)), -1.0)
        rel_w = jnp.dot(rel.astype(BF16), spread, preferred_element_type=F32)
        sel = jnp.where(rel_w == offset, gate_w, 0.0).astype(BF16)
        part = jnp.dot(sel, jnp.concatenate(rows, axis=0), preferred_element_type=F32)
        if c == 0:
            acc_scr[...] = part
        else:
            acc_scr[...] += part

    one_pass(0)
    for c in range(1, MAX_PASSES):
        pl.when(npass_ref[b * N_TOK_TILES + j] > c)(functools.partial(one_pass, c))
    o_ref[0] = x_ref[0] + mod_ref[0, 0, 5:6, :] * acc_scr[...]


def _scatter_call(win, npass, x, y, slot_t, gate_t, modsel, latent_only):
    bsz = x.shape[0]
    first = 1 if latent_only else 0
    smem = pl.BlockSpec(memory_space=pltpu.SMEM)
    tok = lambda n: pl.BlockSpec((1, TOK_TILE, n), lambda b, i: (b, i + first, 0))
    return pl.pallas_call(
        functools.partial(_scatter_kernel, first_tile=first),
        grid=(bsz, N_TOK_TILES - first),
        in_specs=[smem, smem, tok(D_MODEL),
                  pl.BlockSpec((N_EXPERTS, CAP_ALL, D_MODEL), lambda b, i: (0, b, 0)),
                  tok(LANES), tok(LANES),
                  pl.BlockSpec((1, 1, 8, D_MODEL), lambda b, i: (b, jnp.minimum(i + first, 1), 0, 0))],
        out_specs=pl.BlockSpec((1, TOK_TILE, D_MODEL), lambda b, i: (b, i, 0)),
        out_shape=jax.ShapeDtypeStruct((bsz, T_ALL - first * TOK_TILE, D_MODEL), F32),
        scratch_shapes=[pltpu.VMEM((TOK_TILE, D_MODEL), F32)],
        compiler_params=_params(("parallel", "arbitrary")),
        name="moe_scatter",
    )(win, npass, x, y, slot_t, gate_t, modsel)


def _rope_tables(hd, n_tile):
    rows = SEQ // GRID_W
    row = jnp.repeat(jnp.arange(rows, dtype=jnp.int32), GRID_W)
    col = jnp.tile(jnp.arange(GRID_W, dtype=jnp.int32), rows)
    half = hd // 2
    freqs = 1.0 / (ROPE_THETA ** (jnp.arange(0, half, 2, dtype=F32) / half))

    def angles(pos):
        ang = pos.astype(F32)[:, None] * freqs[None, :]
        return jnp.concatenate([ang, ang], axis=-1)

    ang = jnp.concatenate([angles(row), angles(col)], axis=-1)
    cos = jnp.concatenate([jnp.ones((CTX_LEN, hd), F32), jnp.cos(ang)], axis=0)
    sin = jnp.concatenate([jnp.zeros((CTX_LEN, hd), F32), jnp.sin(ang)], axis=0)
    return jnp.tile(cos, (1, n_tile)), jnp.tile(sin, (1, n_tile))


def _group_mean_matrix(group):
    idx = np.arange(BRANCH_W) // group
    return jnp.asarray((idx[:, None] == idx[None, :]).astype(np.float32) / group, BF16)


def _s5_tables(a_re, a_im, log_dt, b_re, b_im, c_re, c_im, bsz):
    dt = jnp.exp(log_dt.astype(F32))[..., None]
    ar, ai = a_re.astype(F32), a_im.astype(F32)
    mag = jnp.exp(ar * dt)
    lr, li = mag * jnp.cos(ai * dt), mag * jnp.sin(ai * dt)
    den = ar * ar + ai * ai
    fr = ((lr - 1.0) * ar + li * ai) / den
    fi = (li * ar - (lr - 1.0) * ai) / den
    bbr = fr[..., None] * b_re - fi[..., None] * b_im
    bbi = fr[..., None] * b_im + fi[..., None] * b_re
    eye = jnp.eye(S5_G, dtype=F32)
    bd_in = lambda m: jnp.einsum('ldgph,gk->ldghkp', m, eye).reshape(DEPTH, 2, BRANCH_W, S5_STATE)
    bmat = jnp.concatenate([bd_in(bbr), bd_in(bbi)], axis=-1).astype(BF16)
    bd_out = lambda m: jnp.einsum('ldghp,gk->ldgpkh', m, eye).reshape(DEPTH, 2, S5_STATE, BRANCH_W)
    cmat = jnp.concatenate([bd_out(c_re.astype(F32)), -bd_out(c_im.astype(F32))], axis=-2).astype(BF16)
    lam = jnp.concatenate([lr.reshape(DEPTH, 2, 1, S5_STATE), li.reshape(DEPTH, 2, 1, S5_STATE)], axis=-1)
    lam = jnp.broadcast_to(lam, (DEPTH, 2, bsz, 2 * S5_STATE))
    return bmat, lam, cmat


def _mix_weights(w_in):
    rep = B_HEADS // 2
    cut = lambda lo, n: w_in[:, :, lo:lo + n]
    dup = lambda lo: jnp.concatenate([cut(lo + (h // rep) * B_HD, B_HD) for h in range(B_HEADS)], axis=-1)
    dt_cols = jnp.pad(cut(2304, 2 * M2_HEADS), ((0, 0), (0, 0), (0, LANES - 2 * M2_HEADS)))
    w_mix = jnp.concatenate([cut(0, 1024), dup(1024), dup(1152), cut(1280, 1024), dt_cols], axis=-1)
    return w_mix.astype(BF16), w_in[:, :, 2312:].astype(BF16)


def _prepare(bsz, c, c_ctx, w_ada, b_ada, norm_g, w_in, da_q_g, da_k_g, da_lambda, da_sub_g, gqa_q_g, gqa_k_g,
             s5_a_re, s5_a_im, s5_log_dt, s5_b_re, s5_b_im, s5_c_re, s5_c_im, s5_d, s5_w_glu, s5_b_glu,
             m2_conv_w, m2_conv_b, m2_a_log, m2_dt_bias, m2_d, m2_norm_g, w_branch, w_out, w_router):
    tb = {}
    cc = jnp.concatenate([c, c_ctx[None], jnp.zeros((16 - bsz - 1, D_MODEL), F32)], axis=0)
    mod_all = _ada_call(cc, w_ada, b_ada).reshape(DEPTH, 16, 6, D_MODEL)
    mod_all = jnp.pad(mod_all, ((0, 0), (0, 0), (0, 2), (0, 0)))
    tb["modsel"] = jnp.stack([jnp.broadcast_to(mod_all[:, bsz:bsz + 1], (DEPTH, bsz, 8, D_MODEL)),
                              mod_all[:, :bsz]], axis=2)
    tb["w_mix"], tb["w_gate_in"] = _mix_weights(w_in)
    tb["cosa"], tb["sina"] = _rope_tables(A_HD, BRANCH_W // A_HD)
    tb["cosb"], tb["sinb"] = _rope_tables(B_HD, BRANCH_W // B_HD)
    tb["gmat_a"], tb["gmat_b"] = _group_mean_matrix(A_HD), _group_mean_matrix(B_HD)
    tile = lambda g: jnp.tile(g.astype(F32), (1, BRANCH_W // g.shape[-1]))
    gains = jnp.stack([tile(da_q_g), tile(da_k_g), tile(gqa_q_g), tile(gqa_k_g)], axis=1)
    tb["gains"] = jnp.pad(gains, ((0, 0), (0, 4), (0, 0)))
    amax = lambda g: jnp.max(jnp.abs(g.astype(F32)), axis=-1)
    tb["logit_bounds"] = 1.05 * jnp.stack([math.sqrt(A_HD) * amax(da_q_g) * amax(da_k_g),
                                           math.sqrt(B_HD) * amax(gqa_q_g) * amax(gqa_k_g)], axis=1).reshape(-1)
    lf = da_lambda.astype(F32)
    tb["lam_init"] = [0.8 - 0.6 * math.exp(-0.3 * l) for l in range(DEPTH)]
    tb["lam"] = (jnp.exp(jnp.sum(lf[:, 0] * lf[:, 1], axis=-1)) - jnp.exp(jnp.sum(lf[:, 2] * lf[:, 3], axis=-1))
                 + jnp.asarray(tb["lam_init"], F32)).reshape(DEPTH, 1, 1)
    row3 = lambda t: t.astype(F32).reshape(DEPTH, 1, t.shape[-1])
    tb["sub_g"] = row3(tile(da_sub_g))
    tb["s5_bmat"], tb["s5_lam"], tb["s5_cmat"] = _s5_tables(
        s5_a_re, s5_a_im, s5_log_dt, s5_b_re, s5_b_im, s5_c_re, s5_c_im, bsz)
    tb["s5_d"], tb["s5_b_glu"], tb["s5_w_glu"] = row3(s5_d), row3(s5_b_glu), s5_w_glu.astype(BF16)
    lane_row = lambda t: jnp.pad(t.astype(F32).reshape(DEPTH, 1, 2 * M2_HEADS),
                                 ((0, 0), (0, 0), (0, LANES - 2 * M2_HEADS)))
    tb["m2_a"] = lane_row(-jnp.exp(m2_a_log.astype(F32)))
    tb["m2_bias"] = lane_row(m2_dt_bias)
    tb["m2_dskip"] = row3(jnp.repeat(m2_d.astype(F32), M2_HD, axis=-1))
    tb["conv_w"] = jnp.pad(m2_conv_w.astype(F32), ((0, 0), (0, 8 - M2_CONV), (0, 0)))
    tb["conv_b"] = row3(m2_conv_b)
    tb["m2_norm_g"] = row3(m2_norm_g)
    tb["norm_rows"] = norm_g.astype(F32).reshape(DEPTH * 2, 1, D_MODEL)
    tb["w_branch"], tb["w_out"] = w_branch.astype(BF16), w_out.astype(BF16)
    tb["w_router_t"] = jnp.swapaxes(w_router.astype(F32), 1, 2)
    return tb


def _mixers(l, xs, tb):
    qa, ka, va, qb, kb, vb, u, z, xbc, dt = _inproj_call(
        l, xs, tb["modsel"][l], tb["norm_rows"], tb["w_mix"], tb["gains"], tb["gmat_a"], tb["gmat_b"],
        tb["cosa"], tb["sina"], tb["cosb"], tb["sinb"])
    ya = _attn_call(l, tb["logit_bounds"], qa, ka, va, tb["lam"], tb["sub_g"], tb["gmat_b"], 2,
                    1.0 - tb["lam_init"][l], "diff_attn")
    yb = _attn_call(l, tb["logit_bounds"], qb, kb, vb, tb["lam"], tb["sub_g"], tb["gmat_b"], 1, None, "gqa_attn")
    y_s5 = _s5_call(u, tb["s5_bmat"][l], tb["s5_lam"][l], tb["s5_cmat"][l])
    ys = _s5_glu_call(l, u, y_s5, tb["s5_d"], tb["s5_w_glu"], tb["s5_b_glu"])
    ssd_args = (xbc, dt, tb["conv_w"], tb["conv_b"], tb["m2_a"], tb["m2_bias"], tb["m2_dskip"])
    y_fwd = _ssd_call(l, 0, *ssd_args)
    ym = _ssd_call(l, 1, *ssd_args, z=z, yprev=y_fwd, norm_g=tb["m2_norm_g"])
    return ya, yb, ys, ym


def _layer(l, xs, tb, w_up, w_gate, w_down, latent_only=False):
    modsel = tb["modsel"][l]
    ya, yb, ys, ym = _mixers(l, xs, tb)
    x_mid, h2, logits_t = _merge_call(l, xs, modsel, tb["norm_rows"], ya, yb, ys, ym,
                                      tb["w_gate_in"], tb["w_branch"], tb["w_out"], tb["w_router_t"])
    slot, slot_t, gate_t, win, npass = _route_call(logits_t)
    win = win[:, :, :N_TOK_TILES].reshape(-1)
    npass = npass[:, 0, :N_TOK_TILES].reshape(-1)
    xe = _gather_call(win, npass, slot, h2)
    ye = _ffn_call(l, xe, w_up, w_gate, w_down)
    return _scatter_call(win, npass, x_mid, ye, slot_t, gate_t, modsel, latent_only)


def kernel(x, c, ctx, c_ctx, w_ada, b_ada, norm_g, w_in, da_q_g, da_k_g, da_lambda, da_sub_g, gqa_q_g, gqa_k_g,
           s5_a_re, s5_a_im, s5_log_dt, s5_b_re, s5_b_im, s5_c_re, s5_c_im, s5_d, s5_w_glu, s5_b_glu,
           m2_conv_w, m2_conv_b, m2_a_log, m2_dt_bias, m2_d, m2_norm_g, w_branch, w_out, w_router,
           w_up, w_gate, w_down):
    bsz = x.shape[0]
    assert x.shape == (bsz, SEQ, D_MODEL) and ctx.shape == (bsz, CTX_LEN, D_MODEL) and bsz == 8
    tb = _prepare(bsz, c, c_ctx, w_ada, b_ada, norm_g, w_in, da_q_g, da_k_g, da_lambda, da_sub_g, gqa_q_g,
                  gqa_k_g, s5_a_re, s5_a_im, s5_log_dt, s5_b_re, s5_b_im, s5_c_re, s5_c_im, s5_d, s5_w_glu,
                  s5_b_glu, m2_conv_w, m2_conv_b, m2_a_log, m2_dt_bias, m2_d, m2_norm_g, w_branch, w_out,
                  w_router)
    xs = jnp.concatenate([ctx, x], axis=1)
    for l in range(DEPTH):
        xs = _layer(l, xs, tb, w_up, w_gate, w_down, latent_only=l == DEPTH - 1)
    return xs
```

```python
import functools
import math

import jax
import jax.numpy as jnp
import numpy as np
from jax import lax
from jax.experimental import pallas as pl
from jax.experimental.pallas import tpu as pltpu

F32 = jnp.float32
BF16 = jnp.bfloat16
HI = lax.Precision.HIGHEST

D_MODEL = 1024
SEQ = 2048
DEPTH = 4
GRID_W = 64
CTX_LEN = 256
T_ALL = CTX_LEN + SEQ
BRANCH_W = 256
ROPE_THETA = 10000.0
EPS = 1e-6
A_HD = 32
A_VD = 64
A_HEADS = 4
B_HEADS = 4
B_HD = 64
S5_H = 16
S5_G = 16
S5_P = 64
S5_STATE = S5_G * S5_P
M2_HD = 64
M2_HEADS = 4
M2_GROUPS = 2
M2_N = 64
M2_CONV = 5
M2_XBC = 512
N_EXPERTS = 16
EXPERT_FF = 2 * D_MODEL
EC_FACTOR = 2
CAP_CTX = EC_FACTOR * CTX_LEN // N_EXPERTS
CAP_LAT = EC_FACTOR * SEQ // N_EXPERTS
CAP_ALL = CAP_CTX + CAP_LAT

LANES = 128
TOK_TILE = 256
ROW_TILE = 768
N_TOK_TILES = T_ALL // TOK_TILE
CHUNK = 128
N_CHUNKS = T_ALL // CHUNK
CTX_CHUNKS = CTX_LEN // CHUNK
MIX_COLS = 2944
FF_TILE = 512
VMEM_LIMIT = 56 * 1024 * 1024
LOG2E = 1.0 / math.log(2.0)
SAFE_LOGIT = 40.0


def _params(sem, vmem=VMEM_LIMIT):
    return pltpu.CompilerParams(dimension_semantics=sem, vmem_limit_bytes=vmem)


def _sigmoid(x):
    return 1.0 / (1.0 + jnp.exp(-x))


def _silu(x):
    return x * _sigmoid(x)


def _rms_rows(x):
    return x * lax.rsqrt(jnp.mean(x * x, axis=-1, keepdims=True) + EPS)


def _ada_kernel(c_ref, w_ref, b_ref, o_ref):
    sc = _silu(c_ref[...])
    o_ref[0] = jnp.dot(sc, w_ref[0], precision=HI, preferred_element_type=F32) + b_ref[0]


def _ada_call(cc, w_ada, b_ada):
    tn = 1536
    return pl.pallas_call(
        _ada_kernel,
        grid=(DEPTH, 6 * D_MODEL // tn),
        in_specs=[
            pl.BlockSpec((16, D_MODEL), lambda l, j: (0, 0)),
            pl.BlockSpec((1, D_MODEL, tn), lambda l, j: (l, 0, j)),
            pl.BlockSpec((1, 1, tn), lambda l, j: (l, 0, j)),
        ],
        out_specs=pl.BlockSpec((1, 16, tn), lambda l, j: (l, 0, j)),
        out_shape=jax.ShapeDtypeStruct((DEPTH, 16, 6 * D_MODEL), F32),
        compiler_params=_params(("arbitrary", "arbitrary")),
        name="ada_mod",
    )(cc, w_ada, b_ada.reshape(DEPTH, 1, 6 * D_MODEL))


def _group_mean(x, gmat):
    hi = x.astype(BF16)
    lo = (x - hi.astype(F32)).astype(BF16)
    return jnp.dot(hi, gmat, preferred_element_type=F32) + jnp.dot(lo, gmat, preferred_element_type=F32)


def _group_norm_rope(x, gmat, gain, cos, sin, shift):
    xn = x * lax.rsqrt(_group_mean(x * x, gmat) + EPS) * gain
    lane = lax.broadcasted_iota(jnp.int32, xn.shape, 1)
    up = pltpu.roll(xn, BRANCH_W - shift, 1)
    dn = pltpu.roll(xn, shift, 1)
    rot = jnp.where((lane % (2 * shift)) < shift, -up, dn)
    return xn * cos + rot * sin


def _inproj_kernel(x_ref, mod_ref, g_ref, w_ref, gains_ref, ga_ref, gb_ref,
                   cosa_ref, sina_ref, cosb_ref, sinb_ref,
                   qa_ref, ka_ref, va_ref, qb_ref, kb_ref, vb_ref, u_ref, z_ref, xbc_ref, dt_ref):
    row = pl.program_id(1) * ROW_TILE + lax.broadcasted_iota(jnp.int32, (ROW_TILE, 1), 0)
    is_ctx = row < CTX_LEN
    mod = lambda k: jnp.where(is_ctx, mod_ref[0, 0, k:k + 1, :], mod_ref[0, 1, k:k + 1, :])
    h = _rms_rows(x_ref[0]) * g_ref[0] * (1.0 + mod(1)) + mod(0)
    p = jnp.dot(h.astype(BF16), w_ref[0], preferred_element_type=F32)
    gains = gains_ref[0]
    w = BRANCH_W
    qa = _group_norm_rope(p[:, 0:w], ga_ref[...], gains[0:1], cosa_ref[...], sina_ref[...], A_HD // 4)
    ka = _group_norm_rope(p[:, w:2 * w], ga_ref[...], gains[1:2], cosa_ref[...], sina_ref[...], A_HD // 4)
    qb = _group_norm_rope(p[:, 3 * w:4 * w], gb_ref[...], gains[2:3], cosb_ref[...], sinb_ref[...], B_HD // 4)
    kb = _group_norm_rope(p[:, 4 * w:5 * w], gb_ref[...], gains[3:4], cosb_ref[...], sinb_ref[...], B_HD // 4)
    qa_ref[0] = (qa * (LOG2E / math.sqrt(A_HD))).astype(BF16)
    ka_ref[0] = ka.astype(BF16)
    va_ref[0] = p[:, 2 * w:3 * w].astype(BF16)
    qb_ref[0] = (qb * (LOG2E / math.sqrt(B_HD))).astype(BF16)
    kb_ref[0] = kb.astype(BF16)
    vb_ref[0] = p[:, 5 * w:6 * w].astype(BF16)
    u_ref[0] = p[:, 6 * w:7 * w].astype(BF16)
    z_ref[0] = p[:, 7 * w:8 * w]
    xbc_ref[0] = p[:, 8 * w:10 * w]
    dt_ref[0] = p[:, 10 * w:10 * w + LANES]


def _inproj_call(l, x, modsel, norm_g, w_mix, gains, gmat_a, gmat_b, cosa, sina, cosb, sinb):
    bsz = x.shape[0]
    w = BRANCH_W
    tok = lambda n: pl.BlockSpec((1, ROW_TILE, n), lambda b, i: (b, i, 0))
    tab = pl.BlockSpec((ROW_TILE, w), lambda b, i: (i, 0))
    full2 = lambda a: pl.BlockSpec(a.shape, lambda b, i: (0, 0))
    out_bt = lambda n, dt: jax.ShapeDtypeStruct((bsz, T_ALL, n), dt)
    return pl.pallas_call(
        _inproj_kernel,
        grid=(bsz, T_ALL // ROW_TILE),
        in_specs=[
            tok(D_MODEL),
            pl.BlockSpec((1, 2, 8, D_MODEL), lambda b, i: (b, 0, 0, 0)),
            pl.BlockSpec((1, 1, D_MODEL), lambda b, i: (2 * l, 0, 0)),
            pl.BlockSpec((1, D_MODEL, MIX_COLS), lambda b, i: (l, 0, 0)),
            pl.BlockSpec((1, 8, w), lambda b, i: (l, 0, 0)),
            full2(gmat_a), full2(gmat_b), tab, tab, tab, tab,
        ],
        out_specs=[tok(w)] * 8 + [tok(M2_XBC), tok(LANES)],
        out_shape=[out_bt(w, BF16)] * 7 + [out_bt(w, F32), out_bt(M2_XBC, F32), out_bt(LANES, F32)],
        compiler_params=_params(("parallel", "parallel")),
        name="in_proj",
    )(x, modsel, norm_g, w_mix, gains, gmat_a, gmat_b, cosa, sina, cosb, sinb)


def _attn_kernel(bound_ref, q_ref, k_ref, v_ref, lam_ref, subg_ref, gmat_ref, o_ref, *,
                 n_maps, post_scale, bound_idx):
    lane = lax.broadcasted_iota(jnp.int32, (TOK_TILE, BRANCH_W), 1)
    q = q_ref[0]
    map_w = A_VD // n_maps

    def attend(n_keys, shift):
        k = k_ref[0, 0:n_keys, :]
        v = v_ref[0, 0:n_keys, :]
        out = jnp.zeros((TOK_TILE, BRANCH_W), F32)
        for head in range(A_HEADS):
            acc = None
            for m in range(n_maps):
                lo = head * A_VD + m * map_w
                qm = jnp.where((lane >= lo) & (lane < lo + map_w), q, jnp.zeros_like(q))
                s = lax.dot_general(qm, k, (((1,), (1,)), ((), ())), preferred_element_type=F32)
                if shift:
                    s = s - jnp.max(s, axis=-1, keepdims=True)
                e = jnp.exp2(s)
                r = 1.0 / jnp.sum(e, axis=-1, keepdims=True)
                o = jnp.dot(e.astype(BF16), v, preferred_element_type=F32)
                acc = o * r if m == 0 else acc - o * (r * lam_ref[0])
            out = jnp.where((lane >= head * A_VD) & (lane < (head + 1) * A_VD), acc, out)
        if post_scale is not None:
            out = out * lax.rsqrt(_group_mean(out * out, gmat_ref[...]) + EPS) * subg_ref[0] * post_scale
        o_ref[0] = out.astype(BF16)

    is_ctx = pl.program_id(1) == 0
    small = bound_ref[bound_idx] <= SAFE_LOGIT
    for n_keys, tile_sel in ((CTX_LEN, is_ctx), (T_ALL, jnp.logical_not(is_ctx))):
        for shift, bound_sel in ((False, small), (True, jnp.logical_not(small))):
            pl.when(jnp.logical_and(tile_sel, bound_sel))(functools.partial(attend, n_keys, shift))


def _attn_call(l, bounds, q, k, v, lam, subg, gmat, n_maps, post_scale, name):
    bsz = q.shape[0]
    w = BRANCH_W
    return pl.pallas_call(
        functools.partial(_attn_kernel, n_maps=n_maps, post_scale=post_scale,
                          bound_idx=2 * l + (0 if n_maps == 2 else 1)),
        grid=(bsz, N_TOK_TILES),
        in_specs=[
            pl.BlockSpec(memory_space=pltpu.SMEM),
            pl.BlockSpec((1, TOK_TILE, w), lambda b, i: (b, i, 0)),
            pl.BlockSpec((1, T_ALL, w), lambda b, i: (b, 0, 0)),
            pl.BlockSpec((1, T_ALL, w), lambda b, i: (b, 0, 0)),
            pl.BlockSpec((1, 1, 1), lambda b, i: (l, 0, 0)),
            pl.BlockSpec((1, 1, w), lambda b, i: (l, 0, 0)),
            pl.BlockSpec(gmat.shape, lambda b, i: (0, 0)),
        ],
        out_specs=pl.BlockSpec((1, TOK_TILE, w), lambda b, i: (b, i, 0)),
        out_shape=jax.ShapeDtypeStruct((bsz, T_ALL, w), BF16),
        compiler_params=_params(("parallel", "parallel")),
        name=name,
    )(bounds, q, k, v, lam, subg, gmat)


def _scan_chunk_index(direction, j):
    back = jnp.where(j < CTX_CHUNKS, CTX_CHUNKS - 1 - j, N_CHUNKS + CTX_CHUNKS - 1 - j)
    return jnp.where(direction == 0, j, back)


S5_PITCH = CHUNK + 8
N_SLABS = 2 * S5_STATE // LANES


def _s5_kernel(u_ref, bmat_ref, lam_ref, cmat_ref, y_ref, bu_scr, h_scr, *, bsz):
    direction = pl.program_id(0)

    @pl.when(pl.program_id(1) == 0)
    def _():
        h_scr[...] = jnp.zeros_like(h_scr)

    for b in range(bsz):
        bu = jnp.dot(u_ref[b], bmat_ref[0], preferred_element_type=F32)
        for s in range(N_SLABS):
            bu_scr[s, b * S5_PITCH:b * S5_PITCH + CHUNK, :] = bu[:, s * LANES:(s + 1) * LANES]
    lam_re = lam_ref[0, :, 0:S5_STATE]
    lam_im = lam_ref[0, :, S5_STATE:2 * S5_STATE]

    def step(t, carry):
        h_re, h_im = carry
        tt = jnp.where(direction == 0, t, CHUNK - 1 - t)
        rows = pl.ds(tt, bsz, stride=S5_PITCH)
        bu = jnp.concatenate([bu_scr[s, rows, :] for s in range(N_SLABS)], axis=1)
        n_re = lam_re * h_re - lam_im * h_im + bu[:, 0:S5_STATE]
        n_im = lam_re * h_im + lam_im * h_re + bu[:, S5_STATE:2 * S5_STATE]
        for s in range(N_SLABS // 2):
            bu_scr[s, rows, :] = n_re[:, s * LANES:(s + 1) * LANES]
            bu_scr[N_SLABS // 2 + s, rows, :] = n_im[:, s * LANES:(s + 1) * LANES]
        return n_re, n_im

    h_re, h_im = lax.fori_loop(0, CHUNK, step, (h_scr[:, 0:S5_STATE], h_scr[:, S5_STATE:2 * S5_STATE]),
                               unroll=4)
    h_scr[:, 0:S5_STATE] = h_re
    h_scr[:, S5_STATE:2 * S5_STATE] = h_im
    for b in range(bsz):
        states = jnp.concatenate([bu_scr[s, b * S5_PITCH:b * S5_PITCH + CHUNK, :] for s in range(N_SLABS)],
                                 axis=1)
        y_ref[0, b] = jnp.dot(states.astype(BF16), cmat_ref[0], preferred_element_type=F32)


def _s5_call(u, bmat, lam, cmat):
    bsz = u.shape[0]
    return pl.pallas_call(
        functools.partial(_s5_kernel, bsz=bsz),
        grid=(2, N_CHUNKS),
        in_specs=[
            pl.BlockSpec((bsz, CHUNK, BRANCH_W), lambda d, j: (0, _scan_chunk_index(d, j), 0)),
            pl.BlockSpec((1, BRANCH_W, 2 * S5_STATE), lambda d, j: (d, 0, 0)),
            pl.BlockSpec((1, bsz, 2 * S5_STATE), lambda d, j: (d, 0, 0)),
            pl.BlockSpec((1, 2 * S5_STATE, BRANCH_W), lambda d, j: (d, 0, 0)),
        ],
        out_specs=pl.BlockSpec((1, bsz, CHUNK, BRANCH_W), lambda d, j: (d, 0, _scan_chunk_index(d, j), 0)),
        out_shape=jax.ShapeDtypeStruct((2, bsz, T_ALL, BRANCH_W), F32),
        scratch_shapes=[pltpu.VMEM((N_SLABS, bsz * S5_PITCH, LANES), F32), pltpu.VMEM((bsz, 2 * S5_STATE), F32)],
        compiler_params=_params(("arbitrary", "arbitrary")),
        name="s5_scan",
    )(u, bmat, lam, cmat)


def _s5_glu_kernel(u_ref, y_ref, d_ref, w_ref, b_ref, o_ref):
    t = d_ref[0] * u_ref[0].astype(F32) + y_ref[0, 0] + y_ref[1, 0]
    t = 0.5 * t * (1.0 + jnp.tanh(math.sqrt(2.0 / math.pi) * (t + 0.044715 * (t * t * t))))
    gate = jnp.dot(t.astype(BF16), w_ref[0], preferred_element_type=F32) + b_ref[0]
    o_ref[0] = (t * _sigmoid(gate)).astype(BF16)


def _s5_glu_call(l, u, y, s5_d, w_glu, b_glu):
    bsz = u.shape[0]
    w = BRANCH_W
    tok = pl.BlockSpec((1, ROW_TILE, w), lambda b, i: (b, i, 0))
    return pl.pallas_call(
        _s5_glu_kernel,
        grid=(bsz, T_ALL // ROW_TILE),
        in_specs=[
            tok,
            pl.BlockSpec((2, 1, ROW_TILE, w), lambda b, i: (0, b, i, 0)),
            pl.BlockSpec((1, 1, w), lambda b, i: (l, 0, 0)),
            pl.BlockSpec((1, w, w), lambda b, i: (l, 0, 0)),
            pl.BlockSpec((1, 1, w), lambda b, i: (l, 0, 0)),
        ],
        out_specs=tok,
        out_shape=jax.ShapeDtypeStruct((bsz, T_ALL, w), BF16),
        compiler_params=_params(("parallel", "parallel")),
        name="s5_glu",
    )(u, y, s5_d, w_glu, b_glu)


PAD_ROWS = 8


def _softplus(x):
    return jnp.maximum(x, 0.0) + jnp.log(1.0 + jnp.exp(-jnp.abs(x)))


def _ssd_kernel(*refs, direction, final):
    if final:
        (xbc_ref, dt_ref, cw_ref, cb_ref, a_ref, bias_ref, dskip_ref, z_ref, yprev_ref, ng_ref,
         o_ref, pad_scr, xact_scr, dtv_scr, ad_scr, st_scr) = refs
    else:
        (xbc_ref, dt_ref, cw_ref, cb_ref, a_ref, bias_ref, dskip_ref,
         o_ref, pad_scr, xact_scr, dtv_scr, ad_scr, st_scr) = refs
    rev = direction == 1

    lat0 = CTX_LEN + 3 * PAD_ROWS - PAD_ROWS
    pad_scr[...] = jnp.zeros_like(pad_scr)
    pad_scr[PAD_ROWS:PAD_ROWS + CTX_LEN, :] = xbc_ref[0, 0:CTX_LEN, :]
    pad_scr[lat0:lat0 + SEQ, :] = xbc_ref[0, CTX_LEN:T_ALL, :]
    for c in range(N_CHUNKS):
        base = c * CHUNK + (PAD_ROWS if c < CTX_CHUNKS else lat0 - CTX_LEN)
        acc = jnp.broadcast_to(cb_ref[0], (CHUNK, M2_XBC))
        for tap in range(M2_CONV):
            start = base + tap - M2_CONV // 2
            acc = acc + cw_ref[0, tap:tap + 1, :] * pad_scr[start:start + CHUNK, :]
        xact_scr[c * CHUNK:(c + 1) * CHUNK, :] = _silu(acc)

    dtv = _softplus(dt_ref[0] + bias_ref[0])
    dtv_scr[...] = dtv
    ad_scr[...] = dtv * a_ref[0]
    st_scr[...] = jnp.zeros_like(st_scr)

    ti = lax.broadcasted_iota(jnp.int32, (CHUNK, CHUNK), 0)
    si = lax.broadcasted_iota(jnp.int32, (CHUNK, CHUNK), 1)
    tri = jnp.where(ti >= si, 1.0, 0.0).astype(F32)
    keep = (si >= ti) if rev else (ti >= si)

    def chunk_body(j, carry):
        ci = _scan_chunk_index(direction, j)
        rows = pl.ds(pl.multiple_of(ci * CHUNK, CHUNK), CHUNK)
        xa = xact_scr[rows, :]
        dtc = dtv_scr[rows, :]
        adc = ad_scr[rows, :]
        cs = jnp.dot(tri, adc, precision=HI, preferred_element_type=F32)
        tot = cs[CHUNK - 1:CHUNK, :]
        pcs = cs - adc if rev else cs
        pcs_t = jnp.transpose(pcs)
        ys = []
        for grp in range(M2_GROUPS):
            b0 = BRANCH_W + grp * M2_N
            c0 = BRANCH_W + M2_GROUPS * M2_N + grp * M2_N
            bg = xa[:, b0:b0 + M2_N]
            cg = xa[:, c0:c0 + M2_N].astype(BF16)
            gmat = lax.dot_general(cg, bg.astype(BF16), (((1,), (1,)), ((), ())), preferred_element_type=F32)
            for hh in range(M2_HEADS // M2_GROUPS):
                head = grp * (M2_HEADS // M2_GROUPS) + hh
                ch = direction * M2_HEADS + head
                col = pcs[:, ch:ch + 1]
                row = pcs_t[ch:ch + 1, :]
                tot_h = tot[:, ch:ch + 1]
                arg = (row - col) if rev else (col - row)
                lmat = jnp.exp(jnp.where(keep, arg, -jnp.inf))
                xd = (xa[:, head * M2_HD:(head + 1) * M2_HD] * dtc[:, ch:ch + 1]).astype(BF16)
                y_diag = jnp.dot((gmat * lmat).astype(BF16), xd, preferred_element_type=F32)
                off = jnp.exp(tot_h - col) if rev else jnp.exp(col)
                dte = jnp.exp(col) if rev else jnp.exp(tot_h - col)
                state = st_scr[head]
                y_off = jnp.dot(cg, state.astype(BF16), preferred_element_type=F32) * off
                st_scr[head] = jnp.exp(tot_h) * state + lax.dot_general(
                    (bg * dte).astype(BF16), xd, (((0,), (0,)), ((), ())), preferred_element_type=F32)
                ys.append(y_diag + y_off)
        y = jnp.concatenate(ys, axis=-1)
        if final:
            y = y + yprev_ref[0, rows, :]
            g = y * _silu(z_ref[0, rows, :])
            o_ref[0, rows, :] = (_rms_rows(g) * ng_ref[0]).astype(BF16)
        else:
            o_ref[0, rows, :] = y + dskip_ref[0] * xa[:, 0:BRANCH_W]
        return carry

    lax.fori_loop(0, N_CHUNKS, chunk_body, 0)


def _ssd_call(l, direction, xbc, dt, conv_w, conv_b, a_rows, bias_rows, dskip, z=None, yprev=None, norm_g=None):
    bsz = xbc.shape[0]
    final = z is not None
    w = BRANCH_W
    seq = lambda n: pl.BlockSpec((1, T_ALL, n), lambda b: (b, 0, 0))
    lay = lambda r, n: pl.BlockSpec((1, r, n), lambda b: (l, 0, 0))
    in_specs = [seq(M2_XBC), seq(LANES), lay(8, M2_XBC), lay(1, M2_XBC), lay(1, LANES), lay(1, LANES), lay(1, w)]
    args = [xbc, dt, conv_w, conv_b, a_rows, bias_rows, dskip]
    if final:
        in_specs += [seq(w), seq(w), lay(1, w)]
        args += [z, yprev, norm_g]
    return pl.pallas_call(
        functools.partial(_ssd_kernel, direction=direction, final=final),
        grid=(bsz,),
        in_specs=in_specs,
        out_specs=seq(w),
        out_shape=jax.ShapeDtypeStruct((bsz, T_ALL, w), BF16 if final else F32),
        scratch_shapes=[
            pltpu.VMEM((T_ALL + 3 * PAD_ROWS, M2_XBC), F32),
            pltpu.VMEM((T_ALL, M2_XBC), F32),
            pltpu.VMEM((T_ALL, LANES), F32),
            pltpu.VMEM((T_ALL, LANES), F32),
            pltpu.VMEM((M2_HEADS, M2_N, M2_HD), F32),
        ],
        compiler_params=_params(("parallel",)),
        name="ssd_bwd" if final else "ssd_fwd",
    )(*args)


MERGE_TILE = ROW_TILE


def _merge_kernel(x_ref, mod_ref, g_ref, ya_ref, yb_ref, ys_ref, ym_ref, wg_ref, wb_ref, wo_ref, wr_ref,
                  xo_ref, h2_ref, lg_ref):
    row = pl.program_id(1) * MERGE_TILE + lax.broadcasted_iota(jnp.int32, (MERGE_TILE, 1), 0)
    is_ctx = row < CTX_LEN
    mod = lambda k: jnp.where(is_ctx, mod_ref[0, 0, k:k + 1, :], mod_ref[0, 1, k:k + 1, :])
    x = x_ref[0]
    h = (_rms_rows(x) * g_ref[0] * (1.0 + mod(1)) + mod(0)).astype(BF16)
    acc = jnp.zeros((MERGE_TILE, D_MODEL), F32)
    for n, y_ref in enumerate((ya_ref, yb_ref, ys_ref, ym_ref)):
        yv = y_ref[0]
        gate = _sigmoid(jnp.dot(h, wg_ref[0, :, n * D_MODEL:(n + 1) * D_MODEL], preferred_element_type=F32))
        acc = acc + gate * jnp.dot(yv, wb_ref[0, n], preferred_element_type=F32)
    xn = x + mod(2) * jnp.dot(acc.astype(BF16), wo_ref[0], preferred_element_type=F32)
    xo_ref[0] = xn
    h2 = _rms_rows(xn) * g_ref[1] * (1.0 + mod(4)) + mod(3)
    h2_ref[0] = h2.astype(BF16)
    lg_ref[0] = lax.dot_general(wr_ref[0], h2, (((1,), (1,)), ((), ())), precision=HI, preferred_element_type=F32)


def _merge_call(l, x, modsel, norm_g, ya, yb, ys, ym, w_gate_in, w_branch, w_out, w_router_t):
    bsz = x.shape[0]
    w = BRANCH_W
    tok = lambda n: pl.BlockSpec((1, MERGE_TILE, n), lambda b, i: (b, i, 0))
    return pl.pallas_call(
        _merge_kernel,
        grid=(bsz, T_ALL // MERGE_TILE),
        in_specs=[
            tok(D_MODEL),
            pl.BlockSpec((1, 2, 8, D_MODEL), lambda b, i: (b, 0, 0, 0)),
            pl.BlockSpec((2, 1, D_MODEL), lambda b, i: (l, 0, 0)),
            tok(w), tok(w), tok(w), tok(w),
            pl.BlockSpec((1, D_MODEL, 4 * D_MODEL), lambda b, i: (l, 0, 0)),
            pl.BlockSpec((1, 4, w, D_MODEL), lambda b, i: (l, 0, 0, 0)),
            pl.BlockSpec((1, D_MODEL, D_MODEL), lambda b, i: (l, 0, 0)),
            pl.BlockSpec((1, N_EXPERTS, D_MODEL), lambda b, i: (l, 0, 0)),
        ],
        out_specs=[tok(D_MODEL), tok(D_MODEL), pl.BlockSpec((1, N_EXPERTS, MERGE_TILE), lambda b, i: (b, 0, i))],
        out_shape=[jax.ShapeDtypeStruct((bsz, T_ALL, D_MODEL), F32),
                   jax.ShapeDtypeStruct((bsz, T_ALL, D_MODEL), BF16),
                   jax.ShapeDtypeStruct((bsz, N_EXPERTS, T_ALL), F32)],
        compiler_params=_params(("parallel", "parallel")),
        name="merge",
    )(x, modsel, norm_g, ya, yb, ys, ym, w_gate_in, w_branch, w_out, w_router_t)


BISECT_STEPS = 48
SLOT_WIN = 64
SLOT_ALIGN = 16
MAX_PASSES = -(-(CAP_LAT + SLOT_ALIGN - 1) // SLOT_WIN)


def _route_kernel(lg_ref, slot_ref, slot_t_ref, gate_t_ref, win_ref, npass_ref):
    lg = lg_ref[0]
    sh = lg - jnp.max(lg, axis=0, keepdims=True)
    ex = jnp.exp(sh)
    den = jnp.sum(ex, axis=0, keepdims=True)
    aff = ex / den
    logaff = sh - jnp.log(den)
    ri = lax.broadcasted_iota(jnp.int32, (LANES, LANES), 0)
    ci = lax.broadcasted_iota(jnp.int32, (LANES, LANES), 1)
    upper = jnp.where(ri <= ci, 1.0, 0.0).astype(BF16)

    def prefix_exclusive(mask, lo, hi):
        carry = jnp.zeros((N_EXPERTS, 1), F32)
        parts = []
        for blk in range((hi - lo) // LANES):
            m = mask[:, blk * LANES:(blk + 1) * LANES]
            inc = jnp.dot(m.astype(BF16), upper, preferred_element_type=F32)
            parts.append(inc - m + carry)
            carry = carry + inc[:, LANES - 1:LANES]
        return jnp.concatenate(parts, axis=1)

    segments = ((0, CTX_LEN, CAP_CTX, 0), (CTX_LEN, T_ALL, CAP_LAT, CAP_CTX))

    def bisect(_, bounds):
        out = []
        for (lo, hi, cap, _), (low, high) in zip(segments, bounds):
            mid = 0.5 * (low + high)
            cnt = jnp.sum(jnp.where(logaff[:, lo:hi] >= mid, 1.0, 0.0), axis=1, keepdims=True)
            ok = cnt >= cap
            out.append((jnp.where(ok, mid, low), jnp.where(ok, high, mid)))
        return tuple(out)

    start = tuple((jnp.min(logaff[:, lo:hi], axis=1, keepdims=True), jnp.ones((N_EXPERTS, 1), F32))
                  for lo, hi, _, _ in segments)
    bounds = lax.fori_loop(0, BISECT_STEPS, bisect, start)

    def choose(lo, hi, cap, base, low, high):
        seg = logaff[:, lo:hi]
        gt = jnp.where(seg >= high, 1.0, 0.0)
        eq = jnp.where(seg >= low, 1.0, 0.0) - gt
        need = cap - jnp.sum(gt, axis=1, keepdims=True)
        sel = gt + eq * jnp.where(prefix_exclusive(eq, lo, hi) < need, 1.0, 0.0)
        pos = prefix_exclusive(sel, lo, hi) + base
        slot = jnp.where(sel > 0.0, pos, -1.0)
        gate = sel * aff[:, lo:hi]
        slot_ref[0, :, lo:hi] = slot.astype(jnp.int32)
        fill = jnp.zeros((LANES - N_EXPERTS, LANES), F32)
        for blk in range((hi - lo) // LANES):
            cols = slice(blk * LANES, (blk + 1) * LANES)
            rows = slice(lo + blk * LANES, lo + (blk + 1) * LANES)
            slot_t_ref[0, rows, :] = jnp.transpose(jnp.concatenate([slot[:, cols], fill], axis=0))
            gate_t_ref[0, rows, :] = jnp.transpose(jnp.concatenate([gate[:, cols], fill], axis=0))
        tiles = []
        for t in range((hi - lo) // TOK_TILE):
            cols = slice(t * TOK_TILE, (t + 1) * TOK_TILE)
            chosen = sel[:, cols] > 0.0
            first = jnp.min(jnp.where(chosen, pos[:, cols], float(CAP_ALL)), axis=1, keepdims=True)
            last = jnp.max(jnp.where(chosen, pos[:, cols], -1.0), axis=1, keepdims=True)
            start = jnp.minimum(jnp.floor(first * (1.0 / SLOT_ALIGN)) * SLOT_ALIGN, float(CAP_ALL - SLOT_WIN))
            tiles.append((lo // TOK_TILE + t, start, jnp.floor((last - start + SLOT_WIN) * (1.0 / SLOT_WIN))))
        return tiles

    lane = lax.broadcasted_iota(jnp.int32, (N_EXPERTS, LANES), 1)
    win = jnp.zeros((N_EXPERTS, LANES), F32)
    npass = jnp.zeros((N_EXPERTS, LANES), F32)
    for seg_def, (low, high) in zip(segments, bounds):
        for j, start, passes in choose(*seg_def, low, high):
            win = jnp.where(lane == j, start, win)
            npass = jnp.where(lane == j, passes, npass)
    win_ref[0] = win.astype(jnp.int32)
    npass_ref[0] = jnp.broadcast_to(jnp.max(npass, axis=0, keepdims=True), (8, LANES)).astype(jnp.int32)


def _route_call(logits_t):
    bsz = logits_t.shape[0]
    spec = pl.BlockSpec((1, N_EXPERTS, T_ALL), lambda b: (b, 0, 0))
    spec_t = pl.BlockSpec((1, T_ALL, LANES), lambda b: (b, 0, 0))
    return pl.pallas_call(
        _route_kernel,
        grid=(bsz,),
        in_specs=[spec],
        out_specs=[spec, spec_t, spec_t, pl.BlockSpec((1, N_EXPERTS, LANES), lambda b: (b, 0, 0)),
                   pl.BlockSpec((1, 8, LANES), lambda b: (b, 0, 0))],
        out_shape=[jax.ShapeDtypeStruct((bsz, N_EXPERTS, T_ALL), jnp.int32),
                   jax.ShapeDtypeStruct((bsz, T_ALL, LANES), F32),
                   jax.ShapeDtypeStruct((bsz, T_ALL, LANES), F32),
                   jax.ShapeDtypeStruct((bsz, N_EXPERTS, LANES), jnp.int32),
                   jax.ShapeDtypeStruct((bsz, 8, LANES), jnp.int32)],
        compiler_params=_params(("parallel",)),
        name="route",
    )(logits_t)


def _window(win_ref, b, j, e, c):
    low = win_ref[(b * N_EXPERTS + e) * N_TOK_TILES + j] + c * SLOT_WIN
    return pl.multiple_of(jnp.minimum(low, CAP_ALL - SLOT_WIN), SLOT_ALIGN), low


def _gather_kernel(win_ref, npass_ref, slot_ref, h_ref, xs_ref):
    b, j = pl.program_id(0), pl.program_id(1)

    @pl.when(j == 0)
    def _():
        xs_ref[...] = jnp.zeros_like(xs_ref)

    def one_pass(c):
        rows = lax.broadcasted_iota(jnp.int32, (SLOT_WIN, TOK_TILE), 0)
        starts, blocks = [], []
        for e in range(N_EXPERTS):
            start, low = _window(win_ref, b, j, e, c)
            ids = rows + start
            hit = jnp.where(slot_ref[0, e:e + 1, :] == ids, jnp.where(ids >= low, 1.0, 0.0), 0.0)
            starts.append(start)
            blocks.append(hit.astype(BF16))
        picked = jnp.dot(jnp.concatenate(blocks, axis=0), h_ref[0], preferred_element_type=F32)
        for e, start in enumerate(starts):
            xs_ref[e, pl.ds(start, SLOT_WIN), :] += picked[e * SLOT_WIN:(e + 1) * SLOT_WIN].astype(BF16)

    one_pass(0)
    for c in range(1, MAX_PASSES):
        pl.when(npass_ref[b * N_TOK_TILES + j] > c)(functools.partial(one_pass, c))


def _gather_call(win, npass, slot, h2):
    bsz = slot.shape[0]
    smem = pl.BlockSpec(memory_space=pltpu.SMEM)
    return pl.pallas_call(
        _gather_kernel,
        grid=(bsz, N_TOK_TILES),
        in_specs=[smem, smem,
                  pl.BlockSpec((1, N_EXPERTS, TOK_TILE), lambda b, j: (b, 0, j)),
                  pl.BlockSpec((1, TOK_TILE, D_MODEL), lambda b, j: (b, j, 0))],
        out_specs=pl.BlockSpec((N_EXPERTS, CAP_ALL, D_MODEL), lambda b, j: (0, b, 0)),
        out_shape=jax.ShapeDtypeStruct((N_EXPERTS, bsz * CAP_ALL, D_MODEL), BF16),
        compiler_params=_params(("parallel", "arbitrary")),
        name="moe_gather",
    )(win, npass, slot, h2)


def _ffn_kernel(x_ref, wu_ref, wg_ref, wd_ref, y_ref, acc_scr, *, row_tile):
    f = pl.program_id(1)
    wu = wu_ref[0, 0].astype(BF16)
    wg = wg_ref[0, 0].astype(BF16)
    wd = wd_ref[0, 0].astype(BF16)
    n_rows = x_ref.shape[1]

    @pl.when(f == 0)
    def _():
        acc_scr[...] = jnp.zeros_like(acc_scr)

    for r0 in range(0, n_rows, row_tile):
        x = x_ref[0, r0:r0 + row_tile, :]
        up = jnp.dot(x, wu, preferred_element_type=F32)
        gt = jnp.dot(x, wg, preferred_element_type=F32)
        acc_scr[r0:r0 + row_tile, :] += jnp.dot((_silu(gt) * up).astype(BF16), wd, preferred_element_type=F32)

    @pl.when(f == EXPERT_FF // FF_TILE - 1)
    def _():
        y_ref[0] = acc_scr[...].astype(BF16)


def _ffn_call(l, xs, w_up, w_gate, w_down):
    n_rows = xs.shape[1]
    row_tile = n_rows // 4
    return pl.pallas_call(
        functools.partial(_ffn_kernel, row_tile=row_tile),
        grid=(N_EXPERTS, EXPERT_FF // FF_TILE),
        in_specs=[
            pl.BlockSpec((1, n_rows, D_MODEL), lambda e, f: (e, 0, 0)),
            pl.BlockSpec((1, 1, D_MODEL, FF_TILE), lambda e, f: (l, e, 0, f)),
            pl.BlockSpec((1, 1, D_MODEL, FF_TILE), lambda e, f: (l, e, 0, f)),
            pl.BlockSpec((1, 1, FF_TILE, D_MODEL), lambda e, f: (l, e, f, 0)),
        ],
        out_specs=pl.BlockSpec((1, n_rows, D_MODEL), lambda e, f: (e, 0, 0)),
        out_shape=jax.ShapeDtypeStruct(xs.shape, BF16),
        scratch_shapes=[pltpu.VMEM((n_rows, D_MODEL), F32)],
        compiler_params=_params(("parallel", "arbitrary")),
        name="moe_ffn",
    )(xs, w_up, w_gate, w_down)


def _scatter_kernel(win_ref, npass_ref, x_ref, y_ref, slot_ref, gate_ref, mod_ref, o_ref, acc_scr, *, first_tile):
    b, j = pl.program_id(0), pl.program_id(1) + first_tile
    slot_t = slot_ref[0]
    gate_t = gate_ref[0].astype(BF16)
    width = N_EXPERTS * SLOT_WIN
    spread = jnp.where(lax.broadcasted_iota(jnp.int32, (LANES, width), 1) // SLOT_WIN
                       == lax.broadcasted_iota(jnp.int32, (LANES, width), 0), 1.0, 0.0).astype(BF16)
    gate_w = jnp.dot(gate_t, spread, preferred_element_type=F32)
    offset = (lax.broadcasted_iota(jnp.int32, (TOK_TILE, width), 1) % SLOT_WIN).astype(F32)
    lane = lax.broadcasted_iota(jnp.int32, (1, LANES), 1)

    def one_pass(c):
        start_row = jnp.zeros((1, LANES), F32)
        low_row = jnp.zeros((1, LANES), F32)
        rows = []
        for e in range(N_EXPERTS):
            start, low = _window(win_ref, b, j, e, c)
            start_row = jnp.where(lane == e, start.astype(F32), start_row)
            low_row = jnp.where(lane == e, low.astype(F32), low_row)
            rows.append(y_ref[e, pl.ds(start, SLOT_WIN), :])
        rel = jnp.where(slot_t >= low_row, jnp.minimum(slot_t - start_row, float(SLOT_WIN)), -1.0)
        rel_w = jnp.dot(rel.astype(BF16), spread, preferred_element_type=F32)
        sel = jnp.where(rel_w == offset, gate_w, 0.0).astype(BF16)
        part = jnp.dot(sel, jnp.concatenate(rows, axis=0), preferred_element_type=F32)
        if c == 0:
            acc_scr[...] = part
        else:
            acc_scr[...] += part

    one_pass(0)
    for c in range(1, MAX_PASSES):
        pl.when(npass_ref[b * N_TOK_TILES + j] > c)(functools.partial(one_pass, c))
    o_ref[0] = x_ref[0] + mod_ref[0, 0, 5:6, :] * acc_scr[...]


def _scatter_call(win, npass, x, y, slot_t, gate_t, modsel, latent_only):
    bsz = x.shape[0]
    first = 1 if latent_only else 0
    smem = pl.BlockSpec(memory_space=pltpu.SMEM)
    tok = lambda n: pl.BlockSpec((1, TOK_TILE, n), lambda b, i: (b, i + first, 0))
    return pl.pallas_call(
        functools.partial(_scatter_kernel, first_tile=first),
        grid=(bsz, N_TOK_TILES - first),
        in_specs=[smem, smem, tok(D_MODEL),
                  pl.BlockSpec((N_EXPERTS, CAP_ALL, D_MODEL), lambda b, i: (0, b, 0)),
                  tok(LANES), tok(LANES),
                  pl.BlockSpec((1, 1, 8, D_MODEL), lambda b, i: (b, jnp.minimum(i + first, 1), 0, 0))],
        out_specs=pl.BlockSpec((1, TOK_TILE, D_MODEL), lambda b, i: (b, i, 0)),
        out_shape=jax.ShapeDtypeStruct((bsz, T_ALL - first * TOK_TILE, D_MODEL), F32),
        scratch_shapes=[pltpu.VMEM((TOK_TILE, D_MODEL), F32)],
        compiler_params=_params(("parallel", "arbitrary")),
        name="moe_scatter",
    )(win, npass, x, y, slot_t, gate_t, modsel)


def _rope_tables(hd, n_tile):
    rows = SEQ // GRID_W
    row = jnp.repeat(jnp.arange(rows, dtype=jnp.int32), GRID_W)
    col = jnp.tile(jnp.arange(GRID_W, dtype=jnp.int32), rows)
    half = hd // 2
    freqs = 1.0 / (ROPE_THETA ** (jnp.arange(0, half, 2, dtype=F32) / half))

    def angles(pos):
        ang = pos.astype(F32)[:, None] * freqs[None, :]
        return jnp.concatenate([ang, ang], axis=-1)

    ang = jnp.concatenate([angles(row), angles(col)], axis=-1)
    cos = jnp.concatenate([jnp.ones((CTX_LEN, hd), F32), jnp.cos(ang)], axis=0)
    sin = jnp.concatenate([jnp.zeros((CTX_LEN, hd), F32), jnp.sin(ang)], axis=0)
    return jnp.tile(cos, (1, n_tile)), jnp.tile(sin, (1, n_tile))


def _group_mean_matrix(group):
    idx = np.arange(BRANCH_W) // group
    return jnp.asarray((idx[:, None] == idx[None, :]).astype(np.float32) / group, BF16)


def _s5_tables(a_re, a_im, log_dt, b_re, b_im, c_re, c_im, bsz):
    dt = jnp.exp(log_dt.astype(F32))[..., None]
    ar, ai = a_re.astype(F32), a_im.astype(F32)
    mag = jnp.exp(ar * dt)
    lr, li = mag * jnp.cos(ai * dt), mag * jnp.sin(ai * dt)
    den = ar * ar + ai * ai
    fr = ((lr - 1.0) * ar + li * ai) / den
    fi = (li * ar - (lr - 1.0) * ai) / den
    bbr = fr[..., None] * b_re - fi[..., None] * b_im
    bbi = fr[..., None] * b_im + fi[..., None] * b_re
    eye = jnp.eye(S5_G, dtype=F32)
    bd_in = lambda m: jnp.einsum('ldgph,gk->ldghkp', m, eye).reshape(DEPTH, 2, BRANCH_W, S5_STATE)
    bmat = jnp.concatenate([bd_in(bbr), bd_in(bbi)], axis=-1).astype(BF16)
    bd_out = lambda m: jnp.einsum('ldghp,gk->ldgpkh', m, eye).reshape(DEPTH, 2, S5_STATE, BRANCH_W)
    cmat = jnp.concatenate([bd_out(c_re.astype(F32)), -bd_out(c_im.astype(F32))], axis=-2).astype(BF16)
    lam = jnp.concatenate([lr.reshape(DEPTH, 2, 1, S5_STATE), li.reshape(DEPTH, 2, 1, S5_STATE)], axis=-1)
    lam = jnp.broadcast_to(lam, (DEPTH, 2, bsz, 2 * S5_STATE))
    return bmat, lam, cmat


def _mix_weights(w_in):
    rep = B_HEADS // 2
    cut = lambda lo, n: w_in[:, :, lo:lo + n]
    dup = lambda lo: jnp.concatenate([cut(lo + (h // rep) * B_HD, B_HD) for h in range(B_HEADS)], axis=-1)
    dt_cols = jnp.pad(cut(2304, 2 * M2_HEADS), ((0, 0), (0, 0), (0, LANES - 2 * M2_HEADS)))
    w_mix = jnp.concatenate([cut(0, 1024), dup(1024), dup(1152), cut(1280, 1024), dt_cols], axis=-1)
    return w_mix.astype(BF16), w_in[:, :, 2312:].astype(BF16)


def _prepare(bsz, c, c_ctx, w_ada, b_ada, norm_g, w_in, da_q_g, da_k_g, da_lambda, da_sub_g, gqa_q_g, gqa_k_g,
             s5_a_re, s5_a_im, s5_log_dt, s5_b_re, s5_b_im, s5_c_re, s5_c_im, s5_d, s5_w_glu, s5_b_glu,
             m2_conv_w, m2_conv_b, m2_a_log, m2_dt_bias, m2_d, m2_norm_g, w_branch, w_out, w_router):
    tb = {}
    cc = jnp.concatenate([c, c_ctx[None], jnp.zeros((16 - bsz - 1, D_MODEL), F32)], axis=0)
    mod_all = _ada_call(cc, w_ada, b_ada).reshape(DEPTH, 16, 6, D_MODEL)
    mod_all = jnp.pad(mod_all, ((0, 0), (0, 0), (0, 2), (0, 0)))
    tb["modsel"] = jnp.stack([jnp.broadcast_to(mod_all[:, bsz:bsz + 1], (DEPTH, bsz, 8, D_MODEL)),
                              mod_all[:, :bsz]], axis=2)
    tb["w_mix"], tb["w_gate_in"] = _mix_weights(w_in)
    tb["cosa"], tb["sina"] = _rope_tables(A_HD, BRANCH_W // A_HD)
    tb["cosb"], tb["sinb"] = _rope_tables(B_HD, BRANCH_W // B_HD)
    tb["gmat_a"], tb["gmat_b"] = _group_mean_matrix(A_HD), _group_mean_matrix(B_HD)
    tile = lambda g: jnp.tile(g.astype(F32), (1, BRANCH_W // g.shape[-1]))
    gains = jnp.stack([tile(da_q_g), tile(da_k_g), tile(gqa_q_g), tile(gqa_k_g)], axis=1)
    tb["gains"] = jnp.pad(gains, ((0, 0), (0, 4), (0, 0)))
    amax = lambda g: jnp.max(jnp.abs(g.astype(F32)), axis=-1)
    tb["logit_bounds"] = 1.05 * jnp.stack([math.sqrt(A_HD) * amax(da_q_g) * amax(da_k_g),
                                           math.sqrt(B_HD) * amax(gqa_q_g) * amax(gqa_k_g)], axis=1).reshape(-1)
    lf = da_lambda.astype(F32)
    tb["lam_init"] = [0.8 - 0.6 * math.exp(-0.3 * l) for l in range(DEPTH)]
    tb["lam"] = (jnp.exp(jnp.sum(lf[:, 0] * lf[:, 1], axis=-1)) - jnp.exp(jnp.sum(lf[:, 2] * lf[:, 3], axis=-1))
                 + jnp.asarray(tb["lam_init"], F32)).reshape(DEPTH, 1, 1)
    row3 = lambda t: t.astype(F32).reshape(DEPTH, 1, t.shape[-1])
    tb["sub_g"] = row3(tile(da_sub_g))
    tb["s5_bmat"], tb["s5_lam"], tb["s5_cmat"] = _s5_tables(
        s5_a_re, s5_a_im, s5_log_dt, s5_b_re, s5_b_im, s5_c_re, s5_c_im, bsz)
    tb["s5_d"], tb["s5_b_glu"], tb["s5_w_glu"] = row3(s5_d), row3(s5_b_glu), s5_w_glu.astype(BF16)
    lane_row = lambda t: jnp.pad(t.astype(F32).reshape(DEPTH, 1, 2 * M2_HEADS),
                                 ((0, 0), (0, 0), (0, LANES - 2 * M2_HEADS)))
    tb["m2_a"] = lane_row(-jnp.exp(m2_a_log.astype(F32)))
    tb["m2_bias"] = lane_row(m2_dt_bias)
    tb["m2_dskip"] = row3(jnp.repeat(m2_d.astype(F32), M2_HD, axis=-1))
    tb["conv_w"] = jnp.pad(m2_conv_w.astype(F32), ((0, 0), (0, 8 - M2_CONV), (0, 0)))
    tb["conv_b"] = row3(m2_conv_b)
    tb["m2_norm_g"] = row3(m2_norm_g)
    tb["norm_rows"] = norm_g.astype(F32).reshape(DEPTH * 2, 1, D_MODEL)
    tb["w_branch"], tb["w_out"] = w_branch.astype(BF16), w_out.astype(BF16)
    tb["w_router_t"] = jnp.swapaxes(w_router.astype(F32), 1, 2)
    return tb


def _mixers(l, xs, tb):
    qa, ka, va, qb, kb, vb, u, z, xbc, dt = _inproj_call(
        l, xs, tb["modsel"][l], tb["norm_rows"], tb["w_mix"], tb["gains"], tb["gmat_a"], tb["gmat_b"],
        tb["cosa"], tb["sina"], tb["cosb"], tb["sinb"])
    ya = _attn_call(l, tb["logit_bounds"], qa, ka, va, tb["lam"], tb["sub_g"], tb["gmat_b"], 2,
                    1.0 - tb["lam_init"][l], "diff_attn")
    yb = _attn_call(l, tb["logit_bounds"], qb, kb, vb, tb["lam"], tb["sub_g"], tb["gmat_b"], 1, None, "gqa_attn")
    y_s5 = _s5_call(u, tb["s5_bmat"][l], tb["s5_lam"][l], tb["s5_cmat"][l])
    ys = _s5_glu_call(l, u, y_s5, tb["s5_d"], tb["s5_w_glu"], tb["s5_b_glu"])
    ssd_args = (xbc, dt, tb["conv_w"], tb["conv_b"], tb["m2_a"], tb["m2_bias"], tb["m2_dskip"])
    y_fwd = _ssd_call(l, 0, *ssd_args)
    ym = _ssd_call(l, 1, *ssd_args, z=z, yprev=y_fwd, norm_g=tb["m2_norm_g"])
    return ya, yb, ys, ym


def _layer(l, xs, tb, w_up, w_gate, w_down, latent_only=False):
    modsel = tb["modsel"][l]
    ya, yb, ys, ym = _mixers(l, xs, tb)
    x_mid, h2, logits_t = _merge_call(l, xs, modsel, tb["norm_rows"], ya, yb, ys, ym,
                                      tb["w_gate_in"], tb["w_branch"], tb["w_out"], tb["w_router_t"])
    slot, slot_t, gate_t, win, npass = _route_call(logits_t)
    win = win[:, :, :N_TOK_TILES].reshape(-1)
    npass = npass[:, 0, :N_TOK_TILES].reshape(-1)
    xe = _gather_call(win, npass, slot, h2)
    ye = _ffn_call(l, xe, w_up, w_gate, w_down)
    return _scatter_call(win, npass, x_mid, ye, slot_t, gate_t, modsel, latent_only)


def kernel(x, c, ctx, c_ctx, w_ada, b_ada, norm_g, w_in, da_q_g, da_k_g, da_lambda, da_sub_g, gqa_q_g, gqa_k_g,
           s5_a_re, s5_a_im, s5_log_dt, s5_b_re, s5_b_im, s5_c_re, s5_c_im, s5_d, s5_w_glu, s5_b_glu,
           m2_conv_w, m2_conv_b, m2_a_log, m2_dt_bias, m2_d, m2_norm_g, w_branch, w_out, w_router,
           w_up, w_gate, w_down):
    bsz = x.shape[0]
    assert x.shape == (bsz, SEQ, D_MODEL) and ctx.shape == (bsz, CTX_LEN, D_MODEL) and bsz == 8
    tb = _prepare(bsz, c, c_ctx, w_ada, b_ada, norm_g, w_in, da_q_g, da_k_g, da_lambda, da_sub_g, gqa_q_g,
                  gqa_k_g, s5_a_re, s5_a_im, s5_log_dt, s5_b_re, s5_b_im, s5_c_re, s5_c_im, s5_d, s5_w_glu,
                  s5_b_glu, m2_conv_w, m2_conv_b, m2_a_log, m2_dt_bias, m2_d, m2_norm_g, w_branch, w_out,
                  w_router)
    xs = jnp.concatenate([ctx, x], axis=1)
    for l in range(DEPTH):
        xs = _layer(l, xs, tb, w_up, w_gate, w_down, latent_only=l == DEPTH - 1)
    return xs
```

```python
import functools
import math

import jax
import jax.numpy as jnp
import numpy as np
from jax import lax
from jax.experimental import pallas as pl
from jax.experimental.pallas import tpu as pltpu

F32 = jnp.float32
BF16 = jnp.bfloat16
HI = lax.Precision.HIGHEST

D_MODEL = 1024
SEQ = 2048
DEPTH = 4
GRID_W = 64
CTX_LEN = 256
T_ALL = CTX_LEN + SEQ
BRANCH_W = 256
ROPE_THETA = 10000.0
EPS = 1e-6
A_HD = 32
A_VD = 64
A_HEADS = 4
B_HEADS = 4
B_HD = 64
S5_H = 16
S5_G = 16
S5_P = 64
S5_STATE = S5_G * S5_P
M2_HD = 64
M2_HEADS = 4
M2_GROUPS = 2
M2_N = 64
M2_CONV = 5
M2_XBC = 512
N_EXPERTS = 16
EXPERT_FF = 2 * D_MODEL
EC_FACTOR = 2
CAP_CTX = EC_FACTOR * CTX_LEN // N_EXPERTS
CAP_LAT = EC_FACTOR * SEQ // N_EXPERTS
CAP_ALL = CAP_CTX + CAP_LAT

LANES = 128
TOK_TILE = 256
ROW_TILE = 768
N_TOK_TILES = T_ALL // TOK_TILE
CHUNK = 128
N_CHUNKS = T_ALL // CHUNK
CTX_CHUNKS = CTX_LEN // CHUNK
MIX_COLS = 2944
FF_TILE = 512
VMEM_LIMIT = 56 * 1024 * 1024
LOG2E = 1.0 / math.log(2.0)
SAFE_LOGIT = 40.0


def _params(sem, vmem=VMEM_LIMIT):
    return pltpu.CompilerParams(dimension_semantics=sem, vmem_limit_bytes=vmem)


def _sigmoid(x):
    return 1.0 / (1.0 + jnp.exp(-x))


def _silu(x):
    return x * _sigmoid(x)


def _rms_rows(x):
    return x * lax.rsqrt(jnp.mean(x * x, axis=-1, keepdims=True) + EPS)


def _ada_kernel(c_ref, w_ref, b_ref, o_ref):
    sc = _silu(c_ref[...])
    o_ref[0] = jnp.dot(sc, w_ref[0], precision=HI, preferred_element_type=F32) + b_ref[0]


def _ada_call(cc, w_ada, b_ada):
    tn = 1536
    return pl.pallas_call(
        _ada_kernel,
        grid=(DEPTH, 6 * D_MODEL // tn),
        in_specs=[
            pl.BlockSpec((16, D_MODEL), lambda l, j: (0, 0)),
            pl.BlockSpec((1, D_MODEL, tn), lambda l, j: (l, 0, j)),
            pl.BlockSpec((1, 1, tn), lambda l, j: (l, 0, j)),
        ],
        out_specs=pl.BlockSpec((1, 16, tn), lambda l, j: (l, 0, j)),
        out_shape=jax.ShapeDtypeStruct((DEPTH, 16, 6 * D_MODEL), F32),
        compiler_params=_params(("arbitrary", "arbitrary")),
        name="ada_mod",
    )(cc, w_ada, b_ada.reshape(DEPTH, 1, 6 * D_MODEL))


def _group_mean(x, gmat):
    hi = x.astype(BF16)
    lo = (x - hi.astype(F32)).astype(BF16)
    return jnp.dot(hi, gmat, preferred_element_type=F32) + jnp.dot(lo, gmat, preferred_element_type=F32)


def _group_norm_rope(x, gmat, gain, cos, sin, shift):
    xn = x * lax.rsqrt(_group_mean(x * x, gmat) + EPS) * gain
    lane = lax.broadcasted_iota(jnp.int32, xn.shape, 1)
    up = pltpu.roll(xn, BRANCH_W - shift, 1)
    dn = pltpu.roll(xn, shift, 1)
    rot = jnp.where((lane % (2 * shift)) < shift, -up, dn)
    return xn * cos + rot * sin


def _inproj_kernel(x_ref, mod_ref, g_ref, w_ref, gains_ref, ga_ref, gb_ref,
                   cosa_ref, sina_ref, cosb_ref, sinb_ref,
                   qa_ref, ka_ref, va_ref, qb_ref, kb_ref, vb_ref, u_ref, z_ref, xbc_ref, dt_ref):
    row = pl.program_id(1) * ROW_TILE + lax.broadcasted_iota(jnp.int32, (ROW_TILE, 1), 0)
    is_ctx = row < CTX_LEN
    mod = lambda k: jnp.where(is_ctx, mod_ref[0, 0, k:k + 1, :], mod_ref[0, 1, k:k + 1, :])
    h = _rms_rows(x_ref[0]) * g_ref[0] * (1.0 + mod(1)) + mod(0)
    p = jnp.dot(h.astype(BF16), w_ref[0], preferred_element_type=F32)
    gains = gains_ref[0]
    w = BRANCH_W
    qa = _group_norm_rope(p[:, 0:w], ga_ref[...], gains[0:1], cosa_ref[...], sina_ref[...], A_HD // 4)
    ka = _group_norm_rope(p[:, w:2 * w], ga_ref[...], gains[1:2], cosa_ref[...], sina_ref[...], A_HD // 4)
    qb = _group_norm_rope(p[:, 3 * w:4 * w], gb_ref[...], gains[2:3], cosb_ref[...], sinb_ref[...], B_HD // 4)
    kb = _group_norm_rope(p[:, 4 * w:5 * w], gb_ref[...], gains[3:4], cosb_ref[...], sinb_ref[...], B_HD // 4)
    qa_ref[0] = (qa * (LOG2E / math.sqrt(A_HD))).astype(BF16)
    ka_ref[0] = ka.astype(BF16)
    va_ref[0] = p[:, 2 * w:3 * w].astype(BF16)
    qb_ref[0] = (qb * (LOG2E / math.sqrt(B_HD))).astype(BF16)
    kb_ref[0] = kb.astype(BF16)
    vb_ref[0] = p[:, 5 * w:6 * w].astype(BF16)
    u_ref[0] = p[:, 6 * w:7 * w].astype(BF16)
    z_ref[0] = p[:, 7 * w:8 * w]
    xbc_ref[0] = p[:, 8 * w:10 * w]
    dt_ref[0] = p[:, 10 * w:10 * w + LANES]


def _inproj_call(l, x, modsel, norm_g, w_mix, gains, gmat_a, gmat_b, cosa, sina, cosb, sinb):
    bsz = x.shape[0]
    w = BRANCH_W
    tok = lambda n: pl.BlockSpec((1, ROW_TILE, n), lambda b, i: (b, i, 0))
    tab = pl.BlockSpec((ROW_TILE, w), lambda b, i: (i, 0))
    full2 = lambda a: pl.BlockSpec(a.shape, lambda b, i: (0, 0))
    out_bt = lambda n, dt: jax.ShapeDtypeStruct((bsz, T_ALL, n), dt)
    return pl.pallas_call(
        _inproj_kernel,
        grid=(bsz, T_ALL // ROW_TILE),
        in_specs=[
            tok(D_MODEL),
            pl.BlockSpec((1, 2, 8, D_MODEL), lambda b, i: (b, 0, 0, 0)),
            pl.BlockSpec((1, 1, D_MODEL), lambda b, i: (2 * l, 0, 0)),
            pl.BlockSpec((1, D_MODEL, MIX_COLS), lambda b, i: (l, 0, 0)),
            pl.BlockSpec((1, 8, w), lambda b, i: (l, 0, 0)),
            full2(gmat_a), full2(gmat_b), tab, tab, tab, tab,
        ],
        out_specs=[tok(w)] * 8 + [tok(M2_XBC), tok(LANES)],
        out_shape=[out_bt(w, BF16)] * 7 + [out_bt(w, F32), out_bt(M2_XBC, F32), out_bt(LANES, F32)],
        compiler_params=_params(("parallel", "parallel")),
        name="in_proj",
    )(x, modsel, norm_g, w_mix, gains, gmat_a, gmat_b, cosa, sina, cosb, sinb)


def _attn_kernel(bound_ref, q_ref, k_ref, v_ref, lam_ref, subg_ref, gmat_ref, o_ref, *,
                 n_maps, post_scale, bound_idx):
    lane = lax.broadcasted_iota(jnp.int32, (TOK_TILE, BRANCH_W), 1)
    q = q_ref[0]
    map_w = A_VD // n_maps

    def attend(n_keys, shift):
        k = k_ref[0, 0:n_keys, :]
        v = v_ref[0, 0:n_keys, :]
        out = jnp.zeros((TOK_TILE, BRANCH_W), F32)
        for head in range(A_HEADS):
            acc = None
            for m in range(n_maps):
                lo = head * A_VD + m * map_w
                qm = jnp.where((lane >= lo) & (lane < lo + map_w), q, jnp.zeros_like(q))
                s = lax.dot_general(qm, k, (((1,), (1,)), ((), ())), preferred_element_type=F32)
                if shift:
                    s = s - jnp.max(s, axis=-1, keepdims=True)
                e = jnp.exp2(s)
                r = 1.0 / jnp.sum(e, axis=-1, keepdims=True)
                o = jnp.dot(e.astype(BF16), v, preferred_element_type=F32)
                acc = o * r if m == 0 else acc - o * (r * lam_ref[0])
            out = jnp.where((lane >= head * A_VD) & (lane < (head + 1) * A_VD), acc, out)
        if post_scale is not None:
            out = out * lax.rsqrt(_group_mean(out * out, gmat_ref[...]) + EPS) * subg_ref[0] * post_scale
        o_ref[0] = out.astype(BF16)

    is_ctx = pl.program_id(1) == 0
    small = bound_ref[bound_idx] <= SAFE_LOGIT
    for n_keys, tile_sel in ((CTX_LEN, is_ctx), (T_ALL, jnp.logical_not(is_ctx))):
        for shift, bound_sel in ((False, small), (True, jnp.logical_not(small))):
            pl.when(jnp.logical_and(tile_sel, bound_sel))(functools.partial(attend, n_keys, shift))


def _attn_call(l, bounds, q, k, v, lam, subg, gmat, n_maps, post_scale, name):
    bsz = q.shape[0]
    w = BRANCH_W
    return pl.pallas_call(
        functools.partial(_attn_kernel, n_maps=n_maps, post_scale=post_scale,
                          bound_idx=2 * l + (0 if n_maps == 2 else 1)),
        grid=(bsz, N_TOK_TILES),
        in_specs=[
            pl.BlockSpec(memory_space=pltpu.SMEM),
            pl.BlockSpec((1, TOK_TILE, w), lambda b, i: (b, i, 0)),
            pl.BlockSpec((1, T_ALL, w), lambda b, i: (b, 0, 0)),
            pl.BlockSpec((1, T_ALL, w), lambda b, i: (b, 0, 0)),
            pl.BlockSpec((1, 1, 1), lambda b, i: (l, 0, 0)),
            pl.BlockSpec((1, 1, w), lambda b, i: (l, 0, 0)),
            pl.BlockSpec(gmat.shape, lambda b, i: (0, 0)),
        ],
        out_specs=pl.BlockSpec((1, TOK_TILE, w), lambda b, i: (b, i, 0)),
        out_shape=jax.ShapeDtypeStruct((bsz, T_ALL, w), BF16),
        compiler_params=_params(("parallel", "parallel")),
        name=name,
    )(bounds, q, k, v, lam, subg, gmat)


def _scan_chunk_index(direction, j):
    back = jnp.where(j < CTX_CHUNKS, CTX_CHUNKS - 1 - j, N_CHUNKS + CTX_CHUNKS - 1 - j)
    return jnp.where(direction == 0, j, back)


S5_PITCH = CHUNK + 8
N_SLABS = 2 * S5_STATE // LANES


def _s5_kernel(u_ref, bmat_ref, lam_ref, cmat_ref, y_ref, bu_scr, h_scr, *, bsz):
    direction = pl.program_id(0)

    @pl.when(pl.program_id(1) == 0)
    def _():
        h_scr[...] = jnp.zeros_like(h_scr)

    for b in range(bsz):
        bu = jnp.dot(u_ref[b], bmat_ref[0], preferred_element_type=F32)
        for s in range(N_SLABS):
            bu_scr[s, b * S5_PITCH:b * S5_PITCH + CHUNK, :] = bu[:, s * LANES:(s + 1) * LANES]
    lam_re = lam_ref[0, :, 0:S5_STATE]
    lam_im = lam_ref[0, :, S5_STATE:2 * S5_STATE]

    def step(t, carry):
        h_re, h_im = carry
        tt = jnp.where(direction == 0, t, CHUNK - 1 - t)
        rows = pl.ds(tt, bsz, stride=S5_PITCH)
        bu = jnp.concatenate([bu_scr[s, rows, :] for s in range(N_SLABS)], axis=1)
        n_re = lam_re * h_re - lam_im * h_im + bu[:, 0:S5_STATE]
        n_im = lam_re * h_im + lam_im * h_re + bu[:, S5_STATE:2 * S5_STATE]
        for s in range(N_SLABS // 2):
            bu_scr[s, rows, :] = n_re[:, s * LANES:(s + 1) * LANES]
            bu_scr[N_SLABS // 2 + s, rows, :] = n_im[:, s * LANES:(s + 1) * LANES]
        return n_re, n_im

    h_re, h_im = lax.fori_loop(0, CHUNK, step, (h_scr[:, 0:S5_STATE], h_scr[:, S5_STATE:2 * S5_STATE]),
                               unroll=4)
    h_scr[:, 0:S5_STATE] = h_re
    h_scr[:, S5_STATE:2 * S5_STATE] = h_im
    for b in range(bsz):
        states = jnp.concatenate([bu_scr[s, b * S5_PITCH:b * S5_PITCH + CHUNK, :] for s in range(N_SLABS)],
                                 axis=1)
        y_ref[0, b] = jnp.dot(states.astype(BF16), cmat_ref[0], preferred_element_type=F32)


def _s5_call(u, bmat, lam, cmat):
    bsz = u.shape[0]
    return pl.pallas_call(
        functools.partial(_s5_kernel, bsz=bsz),
        grid=(2, N_CHUNKS),
        in_specs=[
            pl.BlockSpec((bsz, CHUNK, BRANCH_W), lambda d, j: (0, _scan_chunk_index(d, j), 0)),
            pl.BlockSpec((1, BRANCH_W, 2 * S5_STATE), lambda d, j: (d, 0, 0)),
            pl.BlockSpec((1, bsz, 2 * S5_STATE), lambda d, j: (d, 0, 0)),
            pl.BlockSpec((1, 2 * S5_STATE, BRANCH_W), lambda d, j: (d, 0, 0)),
        ],
        out_specs=pl.BlockSpec((1, bsz, CHUNK, BRANCH_W), lambda d, j: (d, 0, _scan_chunk_index(d, j), 0)),
        out_shape=jax.ShapeDtypeStruct((2, bsz, T_ALL, BRANCH_W), F32),
        scratch_shapes=[pltpu.VMEM((N_SLABS, bsz * S5_PITCH, LANES), F32), pltpu.VMEM((bsz, 2 * S5_STATE), F32)],
        compiler_params=_params(("arbitrary", "arbitrary")),
        name="s5_scan",
    )(u, bmat, lam, cmat)


def _s5_glu_kernel(u_ref, y_ref, d_ref, w_ref, b_ref, o_ref):
    t = d_ref[0] * u_ref[0].astype(F32) + y_ref[0, 0] + y_ref[1, 0]
    t = 0.5 * t * (1.0 + jnp.tanh(math.sqrt(2.0 / math.pi) * (t + 0.044715 * (t * t * t))))
    gate = jnp.dot(t.astype(BF16), w_ref[0], preferred_element_type=F32) + b_ref[0]
    o_ref[0] = (t * _sigmoid(gate)).astype(BF16)


def _s5_glu_call(l, u, y, s5_d, w_glu, b_glu):
    bsz = u.shape[0]
    w = BRANCH_W
    tok = pl.BlockSpec((1, ROW_TILE, w), lambda b, i: (b, i, 0))
    return pl.pallas_call(
        _s5_glu_kernel,
        grid=(bsz, T_ALL // ROW_TILE),
        in_specs=[
            tok,
            pl.BlockSpec((2, 1, ROW_TILE, w), lambda b, i: (0, b, i, 0)),
            pl.BlockSpec((1, 1, w), lambda b, i: (l, 0, 0)),
            pl.BlockSpec((1, w, w), lambda b, i: (l, 0, 0)),
            pl.BlockSpec((1, 1, w), lambda b, i: (l, 0, 0)),
        ],
        out_specs=tok,
        out_shape=jax.ShapeDtypeStruct((bsz, T_ALL, w), BF16),
        compiler_params=_params(("parallel", "parallel")),
        name="s5_glu",
    )(u, y, s5_d, w_glu, b_glu)


PAD_ROWS = 8


def _softplus(x):
    return jnp.maximum(x, 0.0) + jnp.log(1.0 + jnp.exp(-jnp.abs(x)))


def _ssd_kernel(*refs, direction, final):
    if final:
        (xbc_ref, dt_ref, cw_ref, cb_ref, a_ref, bias_ref, dskip_ref, z_ref, yprev_ref, ng_ref,
         o_ref, pad_scr, xact_scr, dtv_scr, ad_scr, st_scr) = refs
    else:
        (xbc_ref, dt_ref, cw_ref, cb_ref, a_ref, bias_ref, dskip_ref,
         o_ref, pad_scr, xact_scr, dtv_scr, ad_scr, st_scr) = refs
    rev = direction == 1

    lat0 = CTX_LEN + 3 * PAD_ROWS - PAD_ROWS
    pad_scr[...] = jnp.zeros_like(pad_scr)
    pad_scr[PAD_ROWS:PAD_ROWS + CTX_LEN, :] = xbc_ref[0, 0:CTX_LEN, :]
    pad_scr[lat0:lat0 + SEQ, :] = xbc_ref[0, CTX_LEN:T_ALL, :]
    for c in range(N_CHUNKS):
        base = c * CHUNK + (PAD_ROWS if c < CTX_CHUNKS else lat0 - CTX_LEN)
        halo = PAD_ROWS
        window = pad_scr[base - halo:base + CHUNK + halo, :].astype(BF16)
        out_row = lax.broadcasted_iota(jnp.int32, (CHUNK, CHUNK + 2 * halo), 0)
        in_row = lax.broadcasted_iota(jnp.int32, (CHUNK, CHUNK + 2 * halo), 1)
        acc = cb_ref[0] + cw_ref[0, M2_CONV // 2:M2_CONV // 2 + 1, :] * pad_scr[base:base + CHUNK, :]
        for tap in range(M2_CONV):
            if tap == M2_CONV // 2:
                continue
            shift = jnp.where(in_row == out_row + (halo + tap - M2_CONV // 2), 1.0, 0.0).astype(BF16)
            acc = acc + cw_ref[0, tap:tap + 1, :] * jnp.dot(shift, window, preferred_element_type=F32)
        xact_scr[c * CHUNK:(c + 1) * CHUNK, :] = _silu(acc)

    dtv = _softplus(dt_ref[0] + bias_ref[0])
    dtv_scr[...] = dtv
    ad_scr[...] = dtv * a_ref[0]
    st_scr[...] = jnp.zeros_like(st_scr)

    ti = lax.broadcasted_iota(jnp.int32, (CHUNK, CHUNK), 0)
    si = lax.broadcasted_iota(jnp.int32, (CHUNK, CHUNK), 1)
    tri = jnp.where(ti >= si, 1.0, 0.0).astype(F32)
    keep = (si >= ti) if rev else (ti >= si)

    def chunk_body(j, carry):
        ci = _scan_chunk_index(direction, j)
        rows = pl.ds(pl.multiple_of(ci * CHUNK, CHUNK), CHUNK)
        xa = xact_scr[rows, :]
        dtc = dtv_scr[rows, :]
        adc = ad_scr[rows, :]
        cs = jnp.dot(tri, adc, precision=HI, preferred_element_type=F32)
        tot = cs[CHUNK - 1:CHUNK, :]
        pcs = cs - adc if rev else cs
        pcs_t = jnp.transpose(pcs)
        ys = []
        for grp in range(M2_GROUPS):
            b0 = BRANCH_W + grp * M2_N
            c0 = BRANCH_W + M2_GROUPS * M2_N + grp * M2_N
            bg = xa[:, b0:b0 + M2_N]
            cg = xa[:, c0:c0 + M2_N].astype(BF16)
            gmat = lax.dot_general(cg, bg.astype(BF16), (((1,), (1,)), ((), ())), preferred_element_type=F32)
            for hh in range(M2_HEADS // M2_GROUPS):
                head = grp * (M2_HEADS // M2_GROUPS) + hh
                ch = direction * M2_HEADS + head
                col = pcs[:, ch:ch + 1]
                row = pcs_t[ch:ch + 1, :]
                tot_h = tot[:, ch:ch + 1]
                arg = (row - col) if rev else (col - row)
                lmat = jnp.exp(jnp.where(keep, arg, -jnp.inf))
                xd = (xa[:, head * M2_HD:(head + 1) * M2_HD] * dtc[:, ch:ch + 1]).astype(BF16)
                y_diag = jnp.dot((gmat * lmat).astype(BF16), xd, preferred_element_type=F32)
                off = jnp.exp(tot_h - col) if rev else jnp.exp(col)
                dte = jnp.exp(col) if rev else jnp.exp(tot_h - col)
                state = st_scr[head]
                y_off = jnp.dot(cg, state.astype(BF16), preferred_element_type=F32) * off
                st_scr[head] = jnp.exp(tot_h) * state + lax.dot_general(
                    (bg * dte).astype(BF16), xd, (((0,), (0,)), ((), ())), preferred_element_type=F32)
                ys.append(y_diag + y_off)
        y = jnp.concatenate(ys, axis=-1)
        if final:
            y = y + yprev_ref[0, rows, :]
            g = y * _silu(z_ref[0, rows, :])
            o_ref[0, rows, :] = (_rms_rows(g) * ng_ref[0]).astype(BF16)
        else:
            o_ref[0, rows, :] = y + dskip_ref[0] * xa[:, 0:BRANCH_W]
        return carry

    lax.fori_loop(0, N_CHUNKS, chunk_body, 0)


def _ssd_call(l, direction, xbc, dt, conv_w, conv_b, a_rows, bias_rows, dskip, z=None, yprev=None, norm_g=None):
    bsz = xbc.shape[0]
    final = z is not None
    w = BRANCH_W
    seq = lambda n: pl.BlockSpec((1, T_ALL, n), lambda b: (b, 0, 0))
    lay = lambda r, n: pl.BlockSpec((1, r, n), lambda b: (l, 0, 0))
    in_specs = [seq(M2_XBC), seq(LANES), lay(8, M2_XBC), lay(1, M2_XBC), lay(1, LANES), lay(1, LANES), lay(1, w)]
    args = [xbc, dt, conv_w, conv_b, a_rows, bias_rows, dskip]
    if final:
        in_specs += [seq(w), seq(w), lay(1, w)]
        args += [z, yprev, norm_g]
    return pl.pallas_call(
        functools.partial(_ssd_kernel, direction=direction, final=final),
        grid=(bsz,),
        in_specs=in_specs,
        out_specs=seq(w),
        out_shape=jax.ShapeDtypeStruct((bsz, T_ALL, w), BF16 if final else F32),
        scratch_shapes=[
            pltpu.VMEM((T_ALL + 3 * PAD_ROWS, M2_XBC), F32),
            pltpu.VMEM((T_ALL, M2_XBC), F32),
            pltpu.VMEM((T_ALL, LANES), F32),
            pltpu.VMEM((T_ALL, LANES), F32),
            pltpu.VMEM((M2_HEADS, M2_N, M2_HD), F32),
        ],
        compiler_params=_params(("parallel",)),
        name="ssd_bwd" if final else "ssd_fwd",
    )(*args)


MERGE_TILE = ROW_TILE


def _merge_kernel(x_ref, mod_ref, g_ref, ya_ref, yb_ref, ys_ref, ym_ref, wg_ref, wb_ref, wo_ref, wr_ref,
                  xo_ref, h2_ref, lg_ref):
    row = pl.program_id(1) * MERGE_TILE + lax.broadcasted_iota(jnp.int32, (MERGE_TILE, 1), 0)
    is_ctx = row < CTX_LEN
    mod = lambda k: jnp.where(is_ctx, mod_ref[0, 0, k:k + 1, :], mod_ref[0, 1, k:k + 1, :])
    x = x_ref[0]
    h = (_rms_rows(x) * g_ref[0] * (1.0 + mod(1)) + mod(0)).astype(BF16)
    acc = jnp.zeros((MERGE_TILE, D_MODEL), F32)
    for n, y_ref in enumerate((ya_ref, yb_ref, ys_ref, ym_ref)):
        yv = y_ref[0]
        gate = _sigmoid(jnp.dot(h, wg_ref[0, :, n * D_MODEL:(n + 1) * D_MODEL], preferred_element_type=F32))
        acc = acc + gate * jnp.dot(yv, wb_ref[0, n], preferred_element_type=F32)
    xn = x + mod(2) * jnp.dot(acc.astype(BF16), wo_ref[0], preferred_element_type=F32)
    xo_ref[0] = xn
    h2 = _rms_rows(xn) * g_ref[1] * (1.0 + mod(4)) + mod(3)
    h2_ref[0] = h2.astype(BF16)
    lg_ref[0] = lax.dot_general(wr_ref[0], h2, (((1,), (1,)), ((), ())), precision=HI, preferred_element_type=F32)


def _merge_call(l, x, modsel, norm_g, ya, yb, ys, ym, w_gate_in, w_branch, w_out, w_router_t):
    bsz = x.shape[0]
    w = BRANCH_W
    tok = lambda n: pl.BlockSpec((1, MERGE_TILE, n), lambda b, i: (b, i, 0))
    return pl.pallas_call(
        _merge_kernel,
        grid=(bsz, T_ALL // MERGE_TILE),
        in_specs=[
            tok(D_MODEL),
            pl.BlockSpec((1, 2, 8, D_MODEL), lambda b, i: (b, 0, 0, 0)),
            pl.BlockSpec((2, 1, D_MODEL), lambda b, i: (l, 0, 0)),
            tok(w), tok(w), tok(w), tok(w),
            pl.BlockSpec((1, D_MODEL, 4 * D_MODEL), lambda b, i: (l, 0, 0)),
            pl.BlockSpec((1, 4, w, D_MODEL), lambda b, i: (l, 0, 0, 0)),
            pl.BlockSpec((1, D_MODEL, D_MODEL), lambda b, i: (l, 0, 0)),
            pl.BlockSpec((1, N_EXPERTS, D_MODEL), lambda b, i: (l, 0, 0)),
        ],
        out_specs=[tok(D_MODEL), tok(D_MODEL), pl.BlockSpec((1, N_EXPERTS, MERGE_TILE), lambda b, i: (b, 0, i))],
        out_shape=[jax.ShapeDtypeStruct((bsz, T_ALL, D_MODEL), F32),
                   jax.ShapeDtypeStruct((bsz, T_ALL, D_MODEL), BF16),
                   jax.ShapeDtypeStruct((bsz, N_EXPERTS, T_ALL), F32)],
        compiler_params=_params(("parallel", "parallel")),
        name="merge",
    )(x, modsel, norm_g, ya, yb, ys, ym, w_gate_in, w_branch, w_out, w_router_t)


BISECT_STEPS = 48
SLOT_WIN = 64
SLOT_ALIGN = 16
MAX_PASSES = -(-(CAP_LAT + SLOT_ALIGN - 1) // SLOT_WIN)


def _route_kernel(lg_ref, slot_ref, slot_t_ref, gate_t_ref, win_ref, npass_ref):
    lg = lg_ref[0]
    sh = lg - jnp.max(lg, axis=0, keepdims=True)
    ex = jnp.exp(sh)
    den = jnp.sum(ex, axis=0, keepdims=True)
    aff = ex / den
    logaff = sh - jnp.log(den)
    ri = lax.broadcasted_iota(jnp.int32, (LANES, LANES), 0)
    ci = lax.broadcasted_iota(jnp.int32, (LANES, LANES), 1)
    upper = jnp.where(ri <= ci, 1.0, 0.0).astype(BF16)

    def prefix_exclusive(mask, lo, hi):
        carry = jnp.zeros((N_EXPERTS, 1), F32)
        parts = []
        for blk in range((hi - lo) // LANES):
            m = mask[:, blk * LANES:(blk + 1) * LANES]
            inc = jnp.dot(m.astype(BF16), upper, preferred_element_type=F32)
            parts.append(inc - m + carry)
            carry = carry + inc[:, LANES - 1:LANES]
        return jnp.concatenate(parts, axis=1)

    segments = ((0, CTX_LEN, CAP_CTX, 0), (CTX_LEN, T_ALL, CAP_LAT, CAP_CTX))

    def bisect(_, bounds):
        out = []
        for (lo, hi, cap, _), (low, high) in zip(segments, bounds):
            mid = 0.5 * (low + high)
            cnt = jnp.sum(jnp.where(logaff[:, lo:hi] >= mid, 1.0, 0.0), axis=1, keepdims=True)
            ok = cnt >= cap
            out.append((jnp.where(ok, mid, low), jnp.where(ok, high, mid)))
        return tuple(out)

    start = tuple((jnp.min(logaff[:, lo:hi], axis=1, keepdims=True), jnp.ones((N_EXPERTS, 1), F32))
                  for lo, hi, _, _ in segments)
    bounds = lax.fori_loop(0, BISECT_STEPS, bisect, start)

    def choose(lo, hi, cap, base, low, high):
        seg = logaff[:, lo:hi]
        gt = jnp.where(seg >= high, 1.0, 0.0)
        eq = jnp.where(seg >= low, 1.0, 0.0) - gt
        need = cap - jnp.sum(gt, axis=1, keepdims=True)
        sel = gt + eq * jnp.where(prefix_exclusive(eq, lo, hi) < need, 1.0, 0.0)
        pos = prefix_exclusive(sel, lo, hi) + base
        slot = jnp.where(sel > 0.0, pos, -1.0)
        gate = sel * aff[:, lo:hi]
        slot_ref[0, :, lo:hi] = slot.astype(jnp.int32)
        fill = jnp.zeros((LANES - N_EXPERTS, LANES), F32)
        for blk in range((hi - lo) // LANES):
            cols = slice(blk * LANES, (blk + 1) * LANES)
            rows = slice(lo + blk * LANES, lo + (blk + 1) * LANES)
            slot_t_ref[0, rows, :] = jnp.transpose(jnp.concatenate([slot[:, cols], fill], axis=0))
            gate_t_ref[0, rows, :] = jnp.transpose(jnp.concatenate([gate[:, cols], fill], axis=0))
        tiles = []
        for t in range((hi - lo) // TOK_TILE):
            cols = slice(t * TOK_TILE, (t + 1) * TOK_TILE)
            chosen = sel[:, cols] > 0.0
            first = jnp.min(jnp.where(chosen, pos[:, cols], float(CAP_ALL)), axis=1, keepdims=True)
            last = jnp.max(jnp.where(chosen, pos[:, cols], -1.0), axis=1, keepdims=True)
            start = jnp.minimum(jnp.floor(first * (1.0 / SLOT_ALIGN)) * SLOT_ALIGN, float(CAP_ALL - SLOT_WIN))
            tiles.append((lo // TOK_TILE + t, start, jnp.floor((last - start + SLOT_WIN) * (1.0 / SLOT_WIN))))
        return tiles

    lane = lax.broadcasted_iota(jnp.int32, (N_EXPERTS, LANES), 1)
    win = jnp.zeros((N_EXPERTS, LANES), F32)
    npass = jnp.zeros((N_EXPERTS, LANES), F32)
    for seg_def, (low, high) in zip(segments, bounds):
        for j, start, passes in choose(*seg_def, low, high):
            win = jnp.where(lane == j, start, win)
            npass = jnp.where(lane == j, passes, npass)
    win_ref[0] = win.astype(jnp.int32)
    npass_ref[0] = jnp.broadcast_to(jnp.max(npass, axis=0, keepdims=True), (8, LANES)).astype(jnp.int32)


def _route_call(logits_t):
    bsz = logits_t.shape[0]
    spec = pl.BlockSpec((1, N_EXPERTS, T_ALL), lambda b: (b, 0, 0))
    spec_t = pl.BlockSpec((1, T_ALL, LANES), lambda b: (b, 0, 0))
    return pl.pallas_call(
        _route_kernel,
        grid=(bsz,),
        in_specs=[spec],
        out_specs=[spec, spec_t, spec_t, pl.BlockSpec((1, N_EXPERTS, LANES), lambda b: (b, 0, 0)),
                   pl.BlockSpec((1, 8, LANES), lambda b: (b, 0, 0))],
        out_shape=[jax.ShapeDtypeStruct((bsz, N_EXPERTS, T_ALL), jnp.int32),
                   jax.ShapeDtypeStruct((bsz, T_ALL, LANES), F32),
                   jax.ShapeDtypeStruct((bsz, T_ALL, LANES), F32),
                   jax.ShapeDtypeStruct((bsz, N_EXPERTS, LANES), jnp.int32),
                   jax.ShapeDtypeStruct((bsz, 8, LANES), jnp.int32)],
        compiler_params=_params(("parallel",)),
        name="route",
    )(logits_t)


def _window(win_ref, b, j, e, c):
    low = win_ref[(b * N_EXPERTS + e) * N_TOK_TILES + j] + c * SLOT_WIN
    return pl.multiple_of(jnp.minimum(low, CAP_ALL - SLOT_WIN), SLOT_ALIGN), low


def _gather_kernel(win_ref, npass_ref, slot_ref, h_ref, xs_ref):
    b, j = pl.program_id(0), pl.program_id(1)

    @pl.when(j == 0)
    def _():
        xs_ref[...] = jnp.zeros_like(xs_ref)

    def one_pass(c):
        rows = lax.broadcasted_iota(jnp.int32, (SLOT_WIN, TOK_TILE), 0)
        starts, blocks = [], []
        for e in range(N_EXPERTS):
            start, low = _window(win_ref, b, j, e, c)
            ids = rows + start
            hit = jnp.where(slot_ref[0, e:e + 1, :] == ids, jnp.where(ids >= low, 1.0, 0.0), 0.0)
            starts.append(start)
            blocks.append(hit.astype(BF16))
        picked = jnp.dot(jnp.concatenate(blocks, axis=0), h_ref[0], preferred_element_type=F32)
        for e, start in enumerate(starts):
            xs_ref[e, pl.ds(start, SLOT_WIN), :] += picked[e * SLOT_WIN:(e + 1) * SLOT_WIN].astype(BF16)

    one_pass(0)
    for c in range(1, MAX_PASSES):
        pl.when(npass_ref[b * N_TOK_TILES + j] > c)(functools.partial(one_pass, c))


def _gather_call(win, npass, slot, h2):
    bsz = slot.shape[0]
    smem = pl.BlockSpec(memory_space=pltpu.SMEM)
    return pl.pallas_call(
        _gather_kernel,
        grid=(bsz, N_TOK_TILES),
        in_specs=[smem, smem,
                  pl.BlockSpec((1, N_EXPERTS, TOK_TILE), lambda b, j: (b, 0, j)),
                  pl.BlockSpec((1, TOK_TILE, D_MODEL), lambda b, j: (b, j, 0))],
        out_specs=pl.BlockSpec((N_EXPERTS, CAP_ALL, D_MODEL), lambda b, j: (0, b, 0)),
        out_shape=jax.ShapeDtypeStruct((N_EXPERTS, bsz * CAP_ALL, D_MODEL), BF16),
        compiler_params=_params(("parallel", "arbitrary")),
        name="moe_gather",
    )(win, npass, slot, h2)


def _ffn_kernel(x_ref, wu_ref, wg_ref, wd_ref, y_ref, acc_scr, *, row_tile):
    f = pl.program_id(1)
    wu = wu_ref[0, 0].astype(BF16)
    wg = wg_ref[0, 0].astype(BF16)
    wd = wd_ref[0, 0].astype(BF16)
    n_rows = x_ref.shape[1]

    @pl.when(f == 0)
    def _():
        acc_scr[...] = jnp.zeros_like(acc_scr)

    for r0 in range(0, n_rows, row_tile):
        x = x_ref[0, r0:r0 + row_tile, :]
        up = jnp.dot(x, wu, preferred_element_type=F32)
        gt = jnp.dot(x, wg, preferred_element_type=F32)
        acc_scr[r0:r0 + row_tile, :] += jnp.dot((_silu(gt) * up).astype(BF16), wd, preferred_element_type=F32)

    @pl.when(f == EXPERT_FF // FF_TILE - 1)
    def _():
        y_ref[0] = acc_scr[...].astype(BF16)


def _ffn_call(l, xs, w_up, w_gate, w_down):
    n_rows = xs.shape[1]
    row_tile = n_rows // 4
    return pl.pallas_call(
        functools.partial(_ffn_kernel, row_tile=row_tile),
        grid=(N_EXPERTS, EXPERT_FF // FF_TILE),
        in_specs=[
            pl.BlockSpec((1, n_rows, D_MODEL), lambda e, f: (e, 0, 0)),
            pl.BlockSpec((1, 1, D_MODEL, FF_TILE), lambda e, f: (l, e, 0, f)),
            pl.BlockSpec((1, 1, D_MODEL, FF_TILE), lambda e, f: (l, e, 0, f)),
            pl.BlockSpec((1, 1, FF_TILE, D_MODEL), lambda e, f: (l, e, f, 0)),
        ],
        out_specs=pl.BlockSpec((1, n_rows, D_MODEL), lambda e, f: (e, 0, 0)),
        out_shape=jax.ShapeDtypeStruct(xs.shape, BF16),
        scratch_shapes=[pltpu.VMEM((n_rows, D_MODEL), F32)],
        compiler_params=_params(("parallel", "arbitrary")),
        name="moe_ffn",
    )(xs, w_up, w_gate, w_down)


def _scatter_kernel(win_ref, npass_ref, x_ref, y_ref, slot_ref, gate_ref, mod_ref, o_ref, acc_scr, *, first_tile):
    b, j = pl.program_id(0), pl.program_id(1) + first_tile
    slot_t = slot_ref[0]
    gate_t = gate_ref[0].astype(BF16)
    width = N_EXPERTS * SLOT_WIN
    spread = jnp.where(lax.broadcasted_iota(jnp.int32, (LANES, width), 1) // SLOT_WIN
                       == lax.broadcasted_iota(jnp.int32, (LANES, width), 0), 1.0, 0.0).astype(BF16)
    gate_w = jnp.dot(gate_t, spread, preferred_element_type=F32)
    offset = (lax.broadcasted_iota(jnp.int32, (TOK_TILE, width), 1) % SLOT_WIN).astype(F32)
    lane = lax.broadcasted_iota(jnp.int32, (1, LANES), 1)

    def one_pass(c):
        start_row = jnp.zeros((1, LANES), F32)
        low_row = jnp.zeros((1, LANES), F32)
        rows = []
        for e in range(N_EXPERTS):
            start, low = _window(win_ref, b, j, e, c)
            start_row = jnp.where(lane == e, start.astype(F32), start_row)
            low_row = jnp.where(lane == e, low.astype(F32), low_row)
            rows.append(y_ref[e, pl.ds(start, SLOT_WIN), :])
        rel = jnp.where(slot_t >= low_row, jnp.minimum(slot_t - start_row, float(SLOT_WIN)), -1.0)
        rel_w = jnp.dot(rel.astype(BF16), spread, preferred_element_type=F32)
        sel = jnp.where(rel_w == offset, gate_w, 0.0).astype(BF16)
        part = jnp.dot(sel, jnp.concatenate(rows, axis=0), preferred_element_type=F32)
        if c == 0:
            acc_scr[...] = part
        else:
            acc_scr[...] += part

    one_pass(0)
    for c in range(1, MAX_PASSES):
        pl.when(npass_ref[b * N_TOK_TILES + j] > c)(functools.partial(one_pass, c))
    o_ref[0] = x_ref[0] + mod_ref[0, 0, 5:6, :] * acc_scr[...]


def _scatter_call(win, npass, x, y, slot_t, gate_t, modsel, latent_only):
    bsz = x.shape[0]
    first = 1 if latent_only else 0
    smem = pl.BlockSpec(memory_space=pltpu.SMEM)
    tok = lambda n: pl.BlockSpec((1, TOK_TILE, n), lambda b, i: (b, i + first, 0))
    return pl.pallas_call(
        functools.partial(_scatter_kernel, first_tile=first),
        grid=(bsz, N_TOK_TILES - first),
        in_specs=[smem, smem, tok(D_MODEL),
                  pl.BlockSpec((N_EXPERTS, CAP_ALL, D_MODEL), lambda b, i: (0, b, 0)),
                  tok(LANES), tok(LANES),
                  pl.BlockSpec((1, 1, 8, D_MODEL), lambda b, i: (b, jnp.minimum(i + first, 1), 0, 0))],
        out_specs=pl.BlockSpec((1, TOK_TILE, D_MODEL), lambda b, i: (b, i, 0)),
        out_shape=jax.ShapeDtypeStruct((bsz, T_ALL - first * TOK_TILE, D_MODEL), F32),
        scratch_shapes=[pltpu.VMEM((TOK_TILE, D_MODEL), F32)],
        compiler_params=_params(("parallel", "arbitrary")),
        name="moe_scatter",
    )(win, npass, x, y, slot_t, gate_t, modsel)


def _rope_tables(hd, n_tile):
    rows = SEQ // GRID_W
    row = jnp.repeat(jnp.arange(rows, dtype=jnp.int32), GRID_W)
    col = jnp.tile(jnp.arange(GRID_W, dtype=jnp.int32), rows)
    half = hd // 2
    freqs = 1.0 / (ROPE_THETA ** (jnp.arange(0, half, 2, dtype=F32) / half))

    def angles(pos):
        ang = pos.astype(F32)[:, None] * freqs[None, :]
        return jnp.concatenate([ang, ang], axis=-1)

    ang = jnp.concatenate([angles(row), angles(col)], axis=-1)
    cos = jnp.concatenate([jnp.ones((CTX_LEN, hd), F32), jnp.cos(ang)], axis=0)
    sin = jnp.concatenate([jnp.zeros((CTX_LEN, hd), F32), jnp.sin(ang)], axis=0)
    return jnp.tile(cos, (1, n_tile)), jnp.tile(sin, (1, n_tile))


def _group_mean_matrix(group):
    idx = np.arange(BRANCH_W) // group
    return jnp.asarray((idx[:, None] == idx[None, :]).astype(np.float32) / group, BF16)


def _s5_tables(a_re, a_im, log_dt, b_re, b_im, c_re, c_im, bsz):
    dt = jnp.exp(log_dt.astype(F32))[..., None]
    ar, ai = a_re.astype(F32), a_im.astype(F32)
    mag = jnp.exp(ar * dt)
    lr, li = mag * jnp.cos(ai * dt), mag * jnp.sin(ai * dt)
    den = ar * ar + ai * ai
    fr = ((lr - 1.0) * ar + li * ai) / den
    fi = (li * ar - (lr - 1.0) * ai) / den
    bbr = fr[..., None] * b_re - fi[..., None] * b_im
    bbi = fr[..., None] * b_im + fi[..., None] * b_re
    eye = jnp.eye(S5_G, dtype=F32)
    bd_in = lambda m: jnp.einsum('ldgph,gk->ldghkp', m, eye).reshape(DEPTH, 2, BRANCH_W, S5_STATE)
    bmat = jnp.concatenate([bd_in(bbr), bd_in(bbi)], axis=-1).astype(BF16)
    bd_out = lambda m: jnp.einsum('ldghp,gk->ldgpkh', m, eye).reshape(DEPTH, 2, S5_STATE, BRANCH_W)
    cmat = jnp.concatenate([bd_out(c_re.astype(F32)), -bd_out(c_im.astype(F32))], axis=-2).astype(BF16)
    lam = jnp.concatenate([lr.reshape(DEPTH, 2, 1, S5_STATE), li.reshape(DEPTH, 2, 1, S5_STATE)], axis=-1)
    lam = jnp.broadcast_to(lam, (DEPTH, 2, bsz, 2 * S5_STATE))
    return bmat, lam, cmat


def _mix_weights(w_in):
    rep = B_HEADS // 2
    cut = lambda lo, n: w_in[:, :, lo:lo + n]
    dup = lambda lo: jnp.concatenate([cut(lo + (h // rep) * B_HD, B_HD) for h in range(B_HEADS)], axis=-1)
    dt_cols = jnp.pad(cut(2304, 2 * M2_HEADS), ((0, 0), (0, 0), (0, LANES - 2 * M2_HEADS)))
    w_mix = jnp.concatenate([cut(0, 1024), dup(1024), dup(1152), cut(1280, 1024), dt_cols], axis=-1)
    return w_mix.astype(BF16), w_in[:, :, 2312:].astype(BF16)


def _prepare(bsz, c, c_ctx, w_ada, b_ada, norm_g, w_in, da_q_g, da_k_g, da_lambda, da_sub_g, gqa_q_g, gqa_k_g,
             s5_a_re, s5_a_im, s5_log_dt, s5_b_re, s5_b_im, s5_c_re, s5_c_im, s5_d, s5_w_glu, s5_b_glu,
             m2_conv_w, m2_conv_b, m2_a_log, m2_dt_bias, m2_d, m2_norm_g, w_branch, w_out, w_router):
    tb = {}
    cc = jnp.concatenate([c, c_ctx[None], jnp.zeros((16 - bsz - 1, D_MODEL), F32)], axis=0)
    mod_all = _ada_call(cc, w_ada, b_ada).reshape(DEPTH, 16, 6, D_MODEL)
    mod_all = jnp.pad(mod_all, ((0, 0), (0, 0), (0, 2), (0, 0)))
    tb["modsel"] = jnp.stack([jnp.broadcast_to(mod_all[:, bsz:bsz + 1], (DEPTH, bsz, 8, D_MODEL)),
                              mod_all[:, :bsz]], axis=2)
    tb["w_mix"], tb["w_gate_in"] = _mix_weights(w_in)
    tb["cosa"], tb["sina"] = _rope_tables(A_HD, BRANCH_W // A_HD)
    tb["cosb"], tb["sinb"] = _rope_tables(B_HD, BRANCH_W // B_HD)
    tb["gmat_a"], tb["gmat_b"] = _group_mean_matrix(A_HD), _group_mean_matrix(B_HD)
    tile = lambda g: jnp.tile(g.astype(F32), (1, BRANCH_W // g.shape[-1]))
    gains = jnp.stack([tile(da_q_g), tile(da_k_g), tile(gqa_q_g), tile(gqa_k_g)], axis=1)
    tb["gains"] = jnp.pad(gains, ((0, 0), (0, 4), (0, 0)))
    amax = lambda g: jnp.max(jnp.abs(g.astype(F32)), axis=-1)
    tb["logit_bounds"] = 1.05 * jnp.stack([math.sqrt(A_HD) * amax(da_q_g) * amax(da_k_g),
                                           math.sqrt(B_HD) * amax(gqa_q_g) * amax(gqa_k_g)], axis=1).reshape(-1)
    lf = da_lambda.astype(F32)
    tb["lam_init"] = [0.8 - 0.6 * math.exp(-0.3 * l) for l in range(DEPTH)]
    tb["lam"] = (jnp.exp(jnp.sum(lf[:, 0] * lf[:, 1], axis=-1)) - jnp.exp(jnp.sum(lf[:, 2] * lf[:, 3], axis=-1))
                 + jnp.asarray(tb["lam_init"], F32)).reshape(DEPTH, 1, 1)
    row3 = lambda t: t.astype(F32).reshape(DEPTH, 1, t.shape[-1])
    tb["sub_g"] = row3(tile(da_sub_g))
    tb["s5_bmat"], tb["s5_lam"], tb["s5_cmat"] = _s5_tables(
        s5_a_re, s5_a_im, s5_log_dt, s5_b_re, s5_b_im, s5_c_re, s5_c_im, bsz)
    tb["s5_d"], tb["s5_b_glu"], tb["s5_w_glu"] = row3(s5_d), row3(s5_b_glu), s5_w_glu.astype(BF16)
    lane_row = lambda t: jnp.pad(t.astype(F32).reshape(DEPTH, 1, 2 * M2_HEADS),
                                 ((0, 0), (0, 0), (0, LANES - 2 * M2_HEADS)))
    tb["m2_a"] = lane_row(-jnp.exp(m2_a_log.astype(F32)))
    tb["m2_bias"] = lane_row(m2_dt_bias)
    tb["m2_dskip"] = row3(jnp.repeat(m2_d.astype(F32), M2_HD, axis=-1))
    tb["conv_w"] = jnp.pad(m2_conv_w.astype(F32), ((0, 0), (0, 8 - M2_CONV), (0, 0)))
    tb["conv_b"] = row3(m2_conv_b)
    tb["m2_norm_g"] = row3(m2_norm_g)
    tb["norm_rows"] = norm_g.astype(F32).reshape(DEPTH * 2, 1, D_MODEL)
    tb["w_branch"], tb["w_out"] = w_branch.astype(BF16), w_out.astype(BF16)
    tb["w_router_t"] = jnp.swapaxes(w_router.astype(F32), 1, 2)
    return tb


def _mixers(l, xs, tb):
    qa, ka, va, qb, kb, vb, u, z, xbc, dt = _inproj_call(
        l, xs, tb["modsel"][l], tb["norm_rows"], tb["w_mix"], tb["gains"], tb["gmat_a"], tb["gmat_b"],
        tb["cosa"], tb["sina"], tb["cosb"], tb["sinb"])
    ya = _attn_call(l, tb["logit_bounds"], qa, ka, va, tb["lam"], tb["sub_g"], tb["gmat_b"], 2,
                    1.0 - tb["lam_init"][l], "diff_attn")
    yb = _attn_call(l, tb["logit_bounds"], qb, kb, vb, tb["lam"], tb["sub_g"], tb["gmat_b"], 1, None, "gqa_attn")
    y_s5 = _s5_call(u, tb["s5_bmat"][l], tb["s5_lam"][l], tb["s5_cmat"][l])
    ys = _s5_glu_call(l, u, y_s5, tb["s5_d"], tb["s5_w_glu"], tb["s5_b_glu"])
    ssd_args = (xbc, dt, tb["conv_w"], tb["conv_b"], tb["m2_a"], tb["m2_bias"], tb["m2_dskip"])
    y_fwd = _ssd_call(l, 0, *ssd_args)
    ym = _ssd_call(l, 1, *ssd_args, z=z, yprev=y_fwd, norm_g=tb["m2_norm_g"])
    return ya, yb, ys, ym


def _layer(l, xs, tb, w_up, w_gate, w_down, latent_only=False):
    modsel = tb["modsel"][l]
    ya, yb, ys, ym = _mixers(l, xs, tb)
    x_mid, h2, logits_t = _merge_call(l, xs, modsel, tb["norm_rows"], ya, yb, ys, ym,
                                      tb["w_gate_in"], tb["w_branch"], tb["w_out"], tb["w_router_t"])
    slot, slot_t, gate_t, win, npass = _route_call(logits_t)
    win = win[:, :, :N_TOK_TILES].reshape(-1)
    npass = npass[:, 0, :N_TOK_TILES].reshape(-1)
    xe = _gather_call(win, npass, slot, h2)
    ye = _ffn_call(l, xe, w_up, w_gate, w_down)
    return _scatter_call(win, npass, x_mid, ye, slot_t, gate_t, modsel, latent_only)


def kernel(x, c, ctx, c_ctx, w_ada, b_ada, norm_g, w_in, da_q_g, da_k_g, da_lambda, da_sub_g, gqa_q_g, gqa_k_g,
           s5_a_re, s5_a_im, s5_log_dt, s5_b_re, s5_b_im, s5_c_re, s5_c_im, s5_d, s5_w_glu, s5_b_glu,
           m2_conv_w, m2_conv_b, m2_a_log, m2_dt_bias, m2_d, m2_norm_g, w_branch, w_out, w_router,
           w_up, w_gate, w_down):
    bsz = x.shape[0]
    assert x.shape == (bsz, SEQ, D_MODEL) and ctx.shape == (bsz, CTX_LEN, D_MODEL) and bsz == 8
    tb = _prepare(bsz, c, c_ctx, w_ada, b_ada, norm_g, w_in, da_q_g, da_k_g, da_lambda, da_sub_g, gqa_q_g,
                  gqa_k_g, s5_a_re, s5_a_im, s5_log_dt, s5_b_re, s5_b_im, s5_c_re, s5_c_im, s5_d, s5_w_glu,
                  s5_b_glu, m2_conv_w, m2_conv_b, m2_a_log, m2_dt_bias, m2_d, m2_norm_g, w_branch, w_out,
                  w_router)
    xs = jnp.concatenate([ctx, x], axis=1)
    for l in range(DEPTH):
        xs = _layer(l, xs, tb, w_up, w_gate, w_down, latent_only=l == DEPTH - 1)
    return xs
```
